```python
import math
import jax
import jax.numpy as jnp
from jax import lax
import numpy as np

D_MODEL = 1024
BATCH = 4
SEQ = 8192
DEPTH = 2

HEAD_DIM = 64
EPS = 1e-6
NEG_INF = -1e30
GRID_W = 64
A_HEADS = 8
A_KV_HEADS = 2
A_GROUP = A_HEADS // A_KV_HEADS
Q_BLOCK = 128
ROPE_THETA = 10000.0
B_GROUPS = 3
B_HEADS_PER_GROUP = 4
B_HEADS = B_GROUPS * B_HEADS_PER_GROUP
B_WINDOWS = (128, 512, 2048)
B_DILATIONS = (1, 4, 16)
N_BUCKETS = 32
MAX_DISTANCE = 1024
C_HEADS = 8
C_HEAD_DIM = 64
C_INNER = C_HEADS * C_HEAD_DIM
C_GROUPS = 2
C_STATE = 128
C_CONV = 5
C_CHUNK = 128
A_Q = A_HEADS * HEAD_DIM
A_KV = A_KV_HEADS * HEAD_DIM
A_OUT = A_Q
B_QKV = B_HEADS * HEAD_DIM
B_OUT = B_HEADS_PER_GROUP * HEAD_DIM
C_BC = C_GROUPS * C_STATE
C_XBC = C_INNER + 2 * C_BC
N_BRANCH = 3
SPLIT_WIDTHS = (A_Q, A_KV, A_KV, A_OUT,
                B_QKV, B_QKV, B_QKV, B_OUT,
                C_INNER, C_INNER, C_BC, C_BC, C_HEADS, C_HEADS,
                N_BRANCH * D_MODEL)
D_IN_PROJ = sum(SPLIT_WIDTHS)
SPLIT_POINTS = tuple(int(v) for v in np.cumsum(SPLIT_WIDTHS)[:-1])

kernel_name = "hybrid_gated_parallel_encoder"


def rms_norm(x, w):
    xf = x.astype(jnp.float32)
    y = xf * lax.rsqrt(jnp.mean(xf * xf, axis=-1, keepdims=True) + EPS)
    return y.astype(x.dtype) * w.astype(x.dtype)


def axial_rope(x, row_idx, col_idx):
    half = x.shape[-1] // 2
    quarter = half // 2
    freqs = ROPE_THETA ** (-jnp.arange(quarter, dtype=jnp.float32) / quarter)

    def rotate(seg, pos):
        ang = pos.astype(jnp.float32)[:, None] * freqs
        cos = jnp.cos(ang)[:, None, :].astype(seg.dtype)
        sin = jnp.sin(ang)[:, None, :].astype(seg.dtype)
        a, b = seg[..., :quarter], seg[..., quarter:]
        return jnp.concatenate([a * cos - b * sin, b * cos + a * sin], axis=-1)

    return jnp.concatenate([rotate(x[..., :half], row_idx), rotate(x[..., half:], col_idx)], axis=-1)


def t5_bucket(rel):
    nb = N_BUCKETS // 2
    max_exact = nb // 2
    ret = jnp.where(rel > 0, nb, 0)
    n = jnp.abs(rel)
    nf = jnp.maximum(n, 1).astype(jnp.float32)
    large = max_exact + (jnp.log(nf / max_exact) / math.log(MAX_DISTANCE / max_exact)
                         * (nb - max_exact)).astype(jnp.int32)
    large = jnp.minimum(large, nb - 1)
    return ret + jnp.where(n < max_exact, n, large)


def mixer_a(q, k, v, q_norm, k_norm, row_idx, col_idx):
    Bsz, S, _ = q.shape
    q = axial_rope(rms_norm(q.reshape(Bsz, S, A_HEADS, HEAD_DIM), q_norm), row_idx, col_idx)
    k = axial_rope(rms_norm(k.reshape(Bsz, S, A_KV_HEADS, HEAD_DIM), k_norm), row_idx, col_idx)
    v = v.reshape(Bsz, S, A_KV_HEADS, HEAD_DIM)
    n_qb = S // Q_BLOCK
    qb = q.reshape(Bsz, n_qb, Q_BLOCK, A_KV_HEADS, A_GROUP, HEAD_DIM).transpose(1, 0, 3, 4, 2, 5)
    k = k.transpose(0, 2, 1, 3)
    v = v.transpose(0, 2, 1, 3)
    scale = HEAD_DIM ** -0.5

    def block(qblk):
        s = jnp.einsum('bkgqe,bkse->bkgqs', qblk, k, preferred_element_type=jnp.float32) * scale
        p = jax.nn.softmax(s, axis=-1)
        return jnp.einsum('bkgqs,bkse->bkgqe', p.astype(v.dtype), v)

    o = lax.map(block, qb)
    return o.transpose(1, 0, 4, 2, 3, 5).reshape(Bsz, S, A_OUT)


def dilated_group_attention(q, k, v, bias_table, dilation, side):
    Bsz, S, H, E = q.shape
    M = S // dilation
    nb = -(-M // side)
    Mp = nb * side

    def phases(t):
        return t.reshape(Bsz, M, dilation, H, E).transpose(0, 2, 3, 1, 4)

    qd = jnp.pad(phases(q), ((0, 0), (0, 0), (0, 0), (0, Mp - M), (0, 0)))
    qd = qd.reshape(Bsz, dilation, H, nb, side, E)

    def band(t):
        t = jnp.pad(phases(t), ((0, 0), (0, 0), (0, 0), (side, side + Mp - M), (0, 0)))
        t = t.reshape(Bsz, dilation, H, nb + 2, side, E)
        return jnp.concatenate([t[:, :, :, :-2], t[:, :, :, 1:-1], t[:, :, :, 2:]], axis=4)

    kw, vw = band(k), band(v)
    qi = jnp.arange(side)[:, None]
    kk = jnp.arange(3 * side)[None, :]
    rel = kk - side - qi
    m_k = jnp.arange(nb)[:, None, None] * side + kk[None] - side
    valid = (jnp.abs(rel) <= side)[None] & (m_k >= 0) & (m_k < M)
    bias = bias_table[t5_bucket(rel * dilation)].transpose(2, 0, 1)
    s = jnp.einsum('bdhnqe,bdhnke->bdhnqk', qd, kw, preferred_element_type=jnp.float32) * (E ** -0.5)
    s = s + bias[None, None, :, None].astype(jnp.float32)
    s = jnp.where(valid, s, NEG_INF)
    mx = jnp.max(s, axis=-1, keepdims=True)
    e = jnp.exp(s - mx)
    den = jnp.sum(e, axis=-1, keepdims=True)
    p = e / den
    lse = (mx + jnp.log(den))[..., 0]
    o = jnp.einsum('bdhnqk,bdhnke->bdhnqe', p.astype(v.dtype), vw)
    o = o.reshape(Bsz, dilation, H, Mp, E)[:, :, :, :M].transpose(0, 3, 1, 2, 4).reshape(Bsz, S, H, E)
    lse = lse.reshape(Bsz, dilation, H, Mp)[..., :M].transpose(0, 3, 1, 2).reshape(Bsz, S, H)
    return o, lse


def mixer_b(q, k, v, q_norm, k_norm, rel_bias):
    Bsz, S, _ = q.shape
    shp = (Bsz, S, B_GROUPS, B_HEADS_PER_GROUP, HEAD_DIM)
    q = rms_norm(q.reshape(shp), q_norm)
    k = rms_norm(k.reshape(shp), k_norm)
    v = v.reshape(shp)
    outs, lses = [], []
    for g in range(B_GROUPS):
        d = B_DILATIONS[g]
        side = B_WINDOWS[g] // (2 * d)
        tbl = rel_bias[:, g * B_HEADS_PER_GROUP:(g + 1) * B_HEADS_PER_GROUP]
        o, l = dilated_group_attention(q[:, :, g], k[:, :, g], v[:, :, g], tbl, d, side)
        outs.append(o)
        lses.append(l)
    w = jax.nn.softmax(jnp.stack(lses, axis=0), axis=0)
    o = jnp.sum(w[..., None].astype(outs[0].dtype) * jnp.stack(outs, axis=0), axis=0)
    return o.reshape(Bsz, S, B_OUT)


def segsum(x):
    T = x.shape[-1]
    xr = jnp.broadcast_to(x[..., None], x.shape + (T,))
    strict = jnp.tril(jnp.ones((T, T), dtype=bool), -1)
    cs = jnp.cumsum(jnp.where(strict, xr, 0.0), axis=-2)
    return jnp.where(jnp.tril(jnp.ones((T, T), dtype=bool)), cs, -jnp.inf)


def ssd_chunked(xdt, a, Bm, Cm):
    b, l, h, p = xdt.shape
    n = Bm.shape[-1]
    c = l // C_CHUNK
    x = xdt.astype(jnp.float32).reshape(b, c, C_CHUNK, h, p)
    Bc = Bm.astype(jnp.float32).reshape(b, c, C_CHUNK, h, n)
    Cc = Cm.astype(jnp.float32).reshape(b, c, C_CHUNK, h, n)
    A = a.astype(jnp.float32).reshape(b, c, C_CHUNK, h).transpose(0, 3, 1, 2)
    A_cs = jnp.cumsum(A, axis=-1)
    Lm = jnp.exp(segsum(A))
    CB = jnp.einsum('bclhn,bcshn->bhcls', Cc, Bc)
    y_diag = jnp.einsum('bhcls,bcshp->bclhp', CB * Lm, x)
    decay_states = jnp.exp(A_cs[..., -1:] - A_cs)
    states = jnp.einsum('bclhn,bhcl,bclhp->bchpn', Bc, decay_states, x)
    states = jnp.concatenate([jnp.zeros_like(states[:, :1]), states], axis=1)
    decay_chunk = jnp.exp(segsum(jnp.pad(A_cs[..., -1], ((0, 0), (0, 0), (1, 0)))))
    states = jnp.einsum('bhzc,bchpn->bzhpn', decay_chunk, states)[:, :-1]
    y_off = jnp.einsum('bclhn,bchpn,bhcl->bclhp', Cc, states, jnp.exp(A_cs))
    return (y_diag + y_off).reshape(b, l, h, p)


def mixer_c(xc, zc, bc, cc, dtf, dtb, conv_w, conv_b, a_log, dt_bias, d_skip, norm_w):
    Bsz, S, _ = xc.shape
    xbc = jnp.concatenate([xc, bc, cc], axis=-1)
    xbc = lax.conv_general_dilated(xbc, conv_w[:, None, :].astype(xbc.dtype), (1,),
                                   [(C_CONV // 2, C_CONV // 2)],
                                   dimension_numbers=('NWC', 'WIO', 'NWC'),
                                   feature_group_count=C_XBC)
    xbc = jax.nn.silu(xbc + conv_b.astype(xbc.dtype))
    xs, Bm, Cm = jnp.split(xbc, [C_INNER, C_INNER + C_BC], axis=-1)
    xs = xs.reshape(Bsz, S, C_HEADS, C_HEAD_DIM)
    rep = C_HEADS // C_GROUPS
    Bm = jnp.repeat(Bm.reshape(Bsz, S, C_GROUPS, C_STATE), rep, axis=2)
    Cm = jnp.repeat(Cm.reshape(Bsz, S, C_GROUPS, C_STATE), rep, axis=2)
    A = -jnp.exp(a_log.astype(jnp.float32))
    dt_f = jax.nn.softplus(dtf.astype(jnp.float32) + dt_bias[0].astype(jnp.float32))
    dt_b = jax.nn.softplus(dtb.astype(jnp.float32) + dt_bias[1].astype(jnp.float32))
    y_f = ssd_chunked(xs * dt_f[..., None], dt_f * A[0], Bm, Cm)
    flip = lambda t: jnp.flip(t, axis=1)
    y_b = flip(ssd_chunked(flip(xs * dt_b[..., None]), flip(dt_b * A[1]), flip(Bm), flip(Cm)))
    y = y_f + y_b + d_skip.astype(jnp.float32)[:, None] * xs
    y = y.reshape(Bsz, S, C_INNER).astype(xc.dtype)
    return rms_norm(y * jax.nn.silu(zc), norm_w)


def setup_inputs(seed: int = 0) -> dict:
    key = jax.random.key(seed)
    ks = jax.random.split(key, 24)
    f32 = jnp.float32

    def nrm(k, shape, s):
        return jax.random.normal(k, shape, f32) * s

    x = nrm(ks[0], (BATCH, SEQ, D_MODEL), 1.0)
    c = nrm(ks[1], (BATCH, D_MODEL), 1.0)
    norm_w = 1.0 + nrm(ks[2], (DEPTH, D_MODEL), 0.1)
    w_ada = nrm(ks[3], (DEPTH, D_MODEL, 3 * D_MODEL), 0.5 * D_MODEL ** -0.5)
    b_ada = nrm(ks[4], (DEPTH, 3 * D_MODEL), 0.02)
    w_in = nrm(ks[5], (DEPTH, D_MODEL, D_IN_PROJ), D_MODEL ** -0.5)
    b_gate = nrm(ks[6], (DEPTH, N_BRANCH * D_MODEL), 0.1)
    q_norm_a = 1.0 + nrm(ks[7], (DEPTH, HEAD_DIM), 0.1)
    k_norm_a = 1.0 + nrm(ks[8], (DEPTH, HEAD_DIM), 0.1)
    q_norm_b = 1.0 + nrm(ks[9], (DEPTH, HEAD_DIM), 0.1)
    k_norm_b = 1.0 + nrm(ks[10], (DEPTH, HEAD_DIM), 0.1)
    rel_bias = nrm(ks[11], (N_BUCKETS, B_HEADS), 0.5)
    conv_w = nrm(ks[12], (DEPTH, C_CONV, C_XBC), C_CONV ** -0.5)
    conv_b = nrm(ks[13], (DEPTH, C_XBC), 0.02)
    a_log = jnp.log(jax.random.uniform(ks[14], (DEPTH, 2, C_HEADS), f32, 1.0, 16.0))
    dt0 = jnp.exp(jax.random.uniform(ks[15], (DEPTH, 2, C_HEADS), f32, math.log(1e-3), math.log(1e-1)))
    dt_bias = dt0 + jnp.log(-jnp.expm1(-dt0))
    d_skip = 1.0 + nrm(ks[16], (DEPTH, C_HEADS), 0.1)
    ssm_norm_w = 1.0 + nrm(ks[17], (DEPTH, C_INNER), 0.1)
    w_proj_a = nrm(ks[18], (DEPTH, A_OUT, D_MODEL), A_OUT ** -0.5)
    w_proj_b = nrm(ks[19], (DEPTH, B_OUT, D_MODEL), B_OUT ** -0.5)
    w_proj_c = nrm(ks[20], (DEPTH, C_INNER, D_MODEL), C_INNER ** -0.5)
    w_out = nrm(ks[21], (DEPTH, D_MODEL, D_MODEL), D_MODEL ** -0.5)
    return {"x": x, "c": c, "norm_w": norm_w, "w_ada": w_ada, "b_ada": b_ada,
            "w_in": w_in, "b_gate": b_gate, "q_norm_a": q_norm_a, "k_norm_a": k_norm_a,
            "q_norm_b": q_norm_b, "k_norm_b": k_norm_b, "rel_bias": rel_bias,
            "conv_w": conv_w, "conv_b": conv_b, "a_log": a_log, "dt_bias": dt_bias,
            "d_skip": d_skip, "ssm_norm_w": ssm_norm_w, "w_proj_a": w_proj_a,
            "w_proj_b": w_proj_b, "w_proj_c": w_proj_c, "w_out": w_out}


def reference(x, c, norm_w, w_ada, b_ada, w_in, b_gate, q_norm_a, k_norm_a, q_norm_b,
              k_norm_b, rel_bias, conv_w, conv_b, a_log, dt_bias, d_skip, ssm_norm_w,
              w_proj_a, w_proj_b, w_proj_c, w_out):
    Bsz, S, _ = x.shape
    rows = S // GRID_W
    row_idx = jnp.repeat(jnp.arange(rows), GRID_W)
    col_idx = jnp.tile(jnp.arange(GRID_W), rows)
    c_act = jax.nn.silu(c)
    for l in range(DEPTH):
        mod = c_act @ w_ada[l] + b_ada[l]
        shift, scale, gate = jnp.split(mod, 3, axis=-1)
        h = rms_norm(x, norm_w[l]) * (1.0 + scale[:, None, :]) + shift[:, None, :]
        proj = h @ w_in[l]
        (qa, ka, va, ga, qb, kb, vb, gb, xc, zc, bc, cc, dtf, dtb,
         mg) = jnp.split(proj, SPLIT_POINTS, axis=-1)
        ya = mixer_a(qa, ka, va, q_norm_a[l], k_norm_a[l], row_idx, col_idx) * jax.nn.silu(ga)
        yb = mixer_b(qb, kb, vb, q_norm_b[l], k_norm_b[l], rel_bias) * jax.nn.silu(gb)
        yc = mixer_c(xc, zc, bc, cc, dtf, dtb, conv_w[l], conv_b[l], a_log[l], dt_bias[l],
                     d_skip[l], ssm_norm_w[l])
        g_a, g_b, g_c = jnp.split(jax.nn.sigmoid(mg + b_gate[l]), N_BRANCH, axis=-1)
        merged = (g_a * (ya @ w_proj_a[l]) + g_b * (yb @ w_proj_b[l])
                  + g_c * (yc @ w_proj_c[l]))
        x = x + gate[:, None, :] * (merged @ w_out[l])
    return x
```

```python
import functools
import math

import numpy as np
import jax
import jax.numpy as jnp
from jax import lax
from jax.experimental import pallas as pl
from jax.experimental.pallas import tpu as pltpu

F32 = jnp.float32
BF16 = jnp.bfloat16

D_MODEL = 1024
HEAD_DIM = 64
EPS = 1e-6
NEG_INF = -1e30
GRID_W = 64
A_HEADS = 8
A_KV_HEADS = 2
ROPE_THETA = 10000.0
B_GROUPS = 3
B_HEADS_PER_GROUP = 4
B_WINDOWS = (128, 512, 2048)
B_DILATIONS = (1, 4, 16)
N_BUCKETS = 32
MAX_DISTANCE = 1024
C_HEADS = 8
C_HEAD_DIM = 64
C_INNER = C_HEADS * C_HEAD_DIM
C_GROUPS = 2
C_STATE = 128
C_CONV = 5
C_CHUNK = 128
N_BRANCH = 3

A_Q = A_HEADS * HEAD_DIM
A_KV = A_KV_HEADS * HEAD_DIM
B_QKV = B_GROUPS * B_HEADS_PER_GROUP * HEAD_DIM
B_OUT = B_HEADS_PER_GROUP * HEAD_DIM
C_BC = C_GROUPS * C_STATE
C_XBC = C_INNER + 2 * C_BC
SPLIT_WIDTHS = (A_Q, A_KV, A_KV, A_Q, B_QKV, B_QKV, B_QKV, B_OUT,
                C_INNER, C_INNER, C_BC, C_BC, C_HEADS, C_HEADS, N_BRANCH * D_MODEL)
SPLIT_OFFSETS = tuple(int(v) for v in np.cumsum((0,) + SPLIT_WIDTHS))

LANES = 128
VMEM_LIMIT = 56 * 1024 * 1024
LOG2E = math.log2(math.e)

_W_GROUPS = (("qa", A_Q), ("ka", A_KV), ("ga", A_Q), ("qb", B_QKV), ("kb", B_QKV), ("vb", B_QKV),
             ("gb", B_OUT), ("xbc", C_XBC), ("zc", C_INNER), ("dt", LANES), ("mg", N_BRANCH * D_MODEL))
_W_OFF = {}
_o = 0
for _n, _w in _W_GROUPS:
    _W_OFF[_n] = (_o, _w)
    _o += _w
W_MAIN_COLS = _o

A_HEAD_ORDER = tuple(h for j in range(4) for h in (j, j + 4))


def _cparams(sem):
    return pltpu.CompilerParams(dimension_semantics=sem, vmem_limit_bytes=VMEM_LIMIT)


def _silu(x):
    return x * jax.nn.sigmoid(x)


def _mod_kernel(c_ref, w_ref, b_ref, o_ref):
    c = c_ref[...]
    o_ref[0] = jnp.dot(_silu(c), w_ref[0], preferred_element_type=F32,
                       precision=lax.Precision.HIGHEST) + b_ref[0]


def _modulation(c_pad, w_ada, b_ada):
    depth, d, n = w_ada.shape
    tn = 512
    return pl.pallas_call(
        _mod_kernel,
        grid=(depth, n // tn),
        in_specs=[pl.BlockSpec((c_pad.shape[0], d), lambda l, j: (0, 0)),
                  pl.BlockSpec((1, d, tn), lambda l, j: (l, 0, j)),
                  pl.BlockSpec((1, 1, tn), lambda l, j: (l, 0, j))],
        out_specs=pl.BlockSpec((1, c_pad.shape[0], tn), lambda l, j: (l, 0, j)),
        out_shape=jax.ShapeDtypeStruct((depth, c_pad.shape[0], n), F32),
        compiler_params=_cparams(("arbitrary", "arbitrary")),
        name="adaln_mod",
    )(c_pad, w_ada, b_ada.reshape(depth, 1, n))


def _head_sumsq_matrix():
    r = lax.broadcasted_iota(jnp.int32, (LANES, LANES), 0) // HEAD_DIM
    c = lax.broadcasted_iota(jnp.int32, (LANES, LANES), 1) // HEAD_DIM
    return jnp.where(r == c, 1.0 / HEAD_DIM, 0.0).astype(BF16)


def _inproj_kernel(x_ref, sc_ref, sh_ref, nw_ref, cos_ref, sin_ref, hw_ref, w_ref, wt_ref,
                   qa_ref, ka_ref, vt_ref, ga_ref, qb_ref, kb_ref, vb_ref, gb_ref,
                   xbc_ref, zc_ref, dt_ref, mg_ref):
    x = x_ref[0]
    ms = jnp.mean(x * x, axis=-1, keepdims=True)
    h = (x * lax.rsqrt(ms + EPS)) * nw_ref[...]
    h = h * sc_ref[0] + sh_ref[0]
    hb = h.astype(BF16)

    bd = _head_sumsq_matrix()
    lane = lax.broadcasted_iota(jnp.int32, (1, LANES), 1)
    first_half = (lane % 32) < 16
    cos = cos_ref[...]
    sin = sin_ref[...]

    def proj(name, c0, c1):
        off, _ = _W_OFF[name]
        return jnp.dot(hb, w_ref[:, off + c0:off + c1], preferred_element_type=F32)

    def head_norm(y, w_row):
        ss = jnp.dot((y * y).astype(BF16), bd, preferred_element_type=F32)
        return (y * lax.rsqrt(ss + EPS)) * w_row

    def rope(y):
        sw = jnp.where(first_half, pltpu.roll(y, LANES - 16, 1), pltpu.roll(y, 16, 1))
        return y * cos + sw * sin

    for j in range(A_Q // LANES):
        y = rope(head_norm(proj("qa", j * LANES, (j + 1) * LANES), hw_ref[0:1, :]))
        qa_ref[0, :, j * LANES:(j + 1) * LANES] = (y * (HEAD_DIM ** -0.5 * LOG2E)).astype(BF16)
    ka_ref[0] = rope(head_norm(proj("ka", 0, LANES), hw_ref[1:2, :])).astype(BF16)
    vt_ref[0] = lax.dot_general(wt_ref[...], hb, (((1,), (1,)), ((), ())),
                                preferred_element_type=F32).astype(BF16)
    for j in range(0, A_Q, 256):
        ga_ref[0, :, j:j + 256] = proj("ga", j, j + 256).astype(BF16)
    for j in range(B_QKV // LANES):
        sl = slice(j * LANES, (j + 1) * LANES)
        qb_ref[0, :, sl] = (head_norm(proj("qb", sl.start, sl.stop), hw_ref[2:3, :])
                            * (HEAD_DIM ** -0.5)).astype(BF16)
        kb_ref[0, :, sl] = head_norm(proj("kb", sl.start, sl.stop), hw_ref[3:4, :]).astype(BF16)
    for j in range(0, B_QKV, 256):
        vb_ref[0, :, j:j + 256] = proj("vb", j, j + 256).astype(BF16)
    gb_ref[0] = proj("gb", 0, B_OUT).astype(BF16)
    for j in range(0, C_XBC, 256):
        xbc_ref[0, :, j:j + 256] = proj("xbc", j, j + 256).astype(BF16)
    for j in range(0, C_INNER, 256):
        zc_ref[0, :, j:j + 256] = proj("zc", j, j + 256).astype(BF16)
    dt_ref[0] = proj("dt", 0, LANES)
    for j in range(0, N_BRANCH * D_MODEL, 512):
        mg_ref[0, :, j:j + 512] = proj("mg", j, j + 512).astype(BF16)


def _in_projection(x, sc1p, shift, norm_w, cos_t, sin_t, head_w, w_main, w_t, tm=256):
    bsz, s, d = x.shape
    nt = s // tm
    row = lambda i, w: pl.BlockSpec((1, tm, w), lambda b, i: (b, i, 0))
    const2 = lambda shp: pl.BlockSpec(shp, lambda b, i: (0, 0))
    outs = [("qa", A_Q, BF16), ("ka", A_KV, BF16), None, ("ga", A_Q, BF16), ("qb", B_QKV, BF16),
            ("kb", B_QKV, BF16), ("vb", B_QKV, BF16), ("gb", B_OUT, BF16), ("xbc", C_XBC, BF16),
            ("zc", C_INNER, BF16), ("dt", LANES, F32), ("mg", N_BRANCH * D_MODEL, BF16)]
    out_shape, out_specs = [], []
    for o in outs:
        if o is None:
            out_shape.append(jax.ShapeDtypeStruct((bsz, A_KV, s), BF16))
            out_specs.append(pl.BlockSpec((1, A_KV, tm), lambda b, i: (b, 0, i)))
        else:
            out_shape.append(jax.ShapeDtypeStruct((bsz, s, o[1]), o[2]))
            out_specs.append(row(0, o[1]))
    return pl.pallas_call(
        _inproj_kernel,
        grid=(bsz, nt),
        in_specs=[row(0, d),
                  pl.BlockSpec((1, 1, d), lambda b, i: (b, 0, 0)),
                  pl.BlockSpec((1, 1, d), lambda b, i: (b, 0, 0)),
                  const2((1, d)),
                  pl.BlockSpec((tm, LANES), lambda b, i: (i, 0)),
                  pl.BlockSpec((tm, LANES), lambda b, i: (i, 0)),
                  const2((8, LANES)),
                  const2(w_main.shape),
                  const2(w_t.shape)],
        out_specs=out_specs,
        out_shape=out_shape,
        compiler_params=_cparams(("arbitrary", "arbitrary")),
        name="norm_inproj",
    )(x, sc1p, shift, norm_w, cos_t, sin_t, head_w, w_main, w_t)


def _attn_a_kernel(q_ref, k_ref, vt_ref, o_ref, *, kc):
    tq = q_ref.shape[1]
    s_len = k_ref.shape[1]
    lane = lax.broadcasted_iota(jnp.int32, (tq, LANES), 1)
    ones = jnp.ones((8, kc), BF16)
    for j in range(A_Q // LANES):
        qblk = q_ref[0, :, j * LANES:(j + 1) * LANES]
        halves = []
        for kv in range(A_KV_HEADS):
            in_half = (lane < HEAD_DIM) if kv == 0 else (lane >= HEAD_DIM)
            qt = jnp.where(in_half, qblk, jnp.zeros_like(qblk))

            def body(c, carry, qt=qt, kv=kv):
                m, acc = carry
                off = pl.multiple_of(c * kc, kc)
                kblk = k_ref[0, pl.ds(off, kc), :]
                st = lax.dot_general(kblk, qt, (((1,), (1,)), ((), ())),
                                     preferred_element_type=F32)
                m_new = jnp.maximum(m, jnp.max(st, axis=0, keepdims=True))
                alpha = jnp.exp2(m - m_new)
                p = jnp.exp2(st - m_new).astype(BF16)
                vaug = jnp.concatenate(
                    [vt_ref[0, kv * HEAD_DIM:(kv + 1) * HEAD_DIM, pl.ds(off, kc)], ones], axis=0)
                acc = alpha * acc + jnp.dot(vaug, p, preferred_element_type=F32)
                return m_new, acc

            m0 = jnp.full((1, tq), NEG_INF, F32)
            acc0 = jnp.zeros((HEAD_DIM + 8, tq), F32)
            _, acc = lax.fori_loop(0, s_len // kc, body, (m0, acc0))
            halves.append(acc[:HEAD_DIM] / acc[HEAD_DIM:HEAD_DIM + 1])
        o_ref[0, :, j * LANES:(j + 1) * LANES] = jnp.concatenate(halves, axis=0).T.astype(o_ref.dtype)


def _mixer_a(qa, ka, vt, tq=256, kc=512):
    bsz, s, _ = qa.shape
    return pl.pallas_call(
        functools.partial(_attn_a_kernel, kc=kc),
        grid=(bsz, s // tq),
        in_specs=[pl.BlockSpec((1, tq, A_Q), lambda b, i: (b, i, 0)),
                  pl.BlockSpec((1, s, A_KV), lambda b, i: (b, 0, 0)),
                  pl.BlockSpec((1, A_KV, s), lambda b, i: (b, 0, 0))],
        out_specs=pl.BlockSpec((1, tq, A_Q), lambda b, i: (b, i, 0)),
        out_shape=jax.ShapeDtypeStruct((bsz, s, A_Q), BF16),
        compiler_params=_cparams(("arbitrary", "arbitrary")),
        name="mixer_a_attn",
    )(qa, ka, vt)


B_SIDE = 64
B_TQ = 256
B_BAND = B_TQ + 2 * B_SIDE


def _attn_b_kernel(q_ref, kp_ref, kc_ref, kn_ref, vp_ref, vc_ref, vn_ref, bias_ref, o_ref, lse_ref):
    mi = pl.program_id(2)
    n_m = pl.num_programs(2)
    q = q_ref[0]
    kband = jnp.concatenate([kp_ref[0], kc_ref[0], kn_ref[0]], axis=0)
    vband = jnp.concatenate([vp_ref[0], vc_ref[0], vn_ref[0]], axis=0)
    kk = lax.broadcasted_iota(jnp.int32, (1, B_BAND), 1)
    key_ok = jnp.logical_and(jnp.logical_or(kk >= B_SIDE, mi > 0),
                             jnp.logical_or(kk < B_SIDE + B_TQ, mi < n_m - 1))
    lane = lax.broadcasted_iota(jnp.int32, (B_TQ, B_OUT), 1) // HEAD_DIM
    o_acc = jnp.zeros((B_TQ, B_OUT), F32)
    lse_acc = jnp.zeros((B_TQ, B_OUT), F32)
    for h in range(B_HEADS_PER_GROUP):
        qh = jnp.where(lane == h, q, jnp.zeros_like(q))
        s = lax.dot_general(qh, kband, (((1,), (1,)), ((), ())), preferred_element_type=F32)
        s = jnp.where(key_ok, s + bias_ref[h], NEG_INF)
        mx = jnp.max(s, axis=-1, keepdims=True)
        e = jnp.exp(s - mx)
        den = jnp.sum(e, axis=-1, keepdims=True)
        p = (e / den).astype(BF16)
        oh = jnp.dot(p, vband, preferred_element_type=F32)
        o_acc = jnp.where(lane == h, oh, o_acc)
        lse_acc = jnp.where(lane == h, mx + jnp.log(den), lse_acc)
    o_ref[0] = o_acc.astype(o_ref.dtype)
    lse_ref[0] = lse_acc


def _mixer_b_group(qb, kb, vb, bias_g, g):
    bsz, s, _ = qb.shape
    d = B_DILATIONS[g]
    m = s // d
    n_m = m // B_TQ
    hb = B_TQ // B_SIDE
    n_side = m // B_SIDE
    view = lambda t: t.reshape(bsz, m, d * B_QKV)
    col = lambda r: r * B_GROUPS + g
    cur = pl.BlockSpec((1, B_TQ, B_OUT), lambda b, r, i: (b, i, col(r)))
    prev = pl.BlockSpec((1, B_SIDE, B_OUT), lambda b, r, i: (b, jnp.maximum(i * hb - 1, 0), col(r)))
    nxt = pl.BlockSpec((1, B_SIDE, B_OUT),
                       lambda b, r, i: (b, jnp.minimum((i + 1) * hb, n_side - 1), col(r)))
    out_spec = pl.BlockSpec((1, B_TQ, B_OUT), lambda b, r, i: (b, i, r))
    o, lse = pl.pallas_call(
        _attn_b_kernel,
        grid=(bsz, d, n_m),
        in_specs=[cur, prev, cur, nxt, prev, cur, nxt,
                  pl.BlockSpec((B_HEADS_PER_GROUP, B_TQ, B_BAND), lambda b, r, i: (0, 0, 0))],
        out_specs=[out_spec, out_spec],
        out_shape=[jax.ShapeDtypeStruct((bsz, m, d * B_OUT), F32),
                   jax.ShapeDtypeStruct((bsz, m, d * B_OUT), F32)],
        compiler_params=_cparams(("arbitrary", "arbitrary", "arbitrary")),
        name=f"mixer_b_attn_g{g}",
    )(view(qb), view(kb), view(kb), view(kb), view(vb), view(vb), view(vb), bias_g)
    return o.reshape(bsz, s, B_OUT), lse.reshape(bsz, s, B_OUT)


def _t5_bucket(rel):
    nb = N_BUCKETS // 2
    max_exact = nb // 2
    ret = jnp.where(rel > 0, nb, 0)
    n = jnp.abs(rel)
    nf = jnp.maximum(n, 1).astype(F32)
    large = max_exact + (jnp.log(nf / max_exact) / math.log(MAX_DISTANCE / max_exact)
                         * (nb - max_exact)).astype(jnp.int32)
    large = jnp.minimum(large, nb - 1)
    return ret + jnp.where(n < max_exact, n, large)


def _band_bias(rel_bias):
    qi = jnp.arange(B_TQ)[:, None]
    kk = jnp.arange(B_BAND)[None, :]
    rel = kk - B_SIDE - qi
    in_band = jnp.abs(rel) <= B_SIDE
    tables = []
    for g in range(B_GROUPS):
        tbl = rel_bias[:, g * B_HEADS_PER_GROUP:(g + 1) * B_HEADS_PER_GROUP]
        bias = tbl[_t5_bucket(rel * B_DILATIONS[g])].transpose(2, 0, 1).astype(F32)
        tables.append(jnp.where(in_band[None], bias, NEG_INF))
    return jnp.stack(tables, axis=0)


CONV_HALO = 8


def _conv_kernel(xp_ref, xc_ref, xn_ref, w_ref, b_ref, o_ref, ext_ref):
    i = pl.program_id(1)
    n_t = pl.num_programs(1)
    tc = xc_ref.shape[1]
    prev = jnp.where(i > 0, xp_ref[0].astype(F32), 0.0)
    nxt = jnp.where(i < n_t - 1, xn_ref[0].astype(F32), 0.0)
    ext_ref[0:CONV_HALO, :] = prev
    ext_ref[CONV_HALO:CONV_HALO + tc, :] = xc_ref[0].astype(F32)
    ext_ref[CONV_HALO + tc:, :] = nxt
    acc = jnp.zeros((tc, xc_ref.shape[2]), F32) + b_ref[...]
    for k in range(C_CONV):
        start = CONV_HALO + k - C_CONV // 2
        acc = acc + ext_ref[start:start + tc, :] * w_ref[k:k + 1, :]
    o_ref[0] = _silu(acc).astype(o_ref.dtype)


def _conv_silu(xbc, conv_w, conv_b, tc=512):
    bsz, s, ch = xbc.shape
    n_t = s // tc
    hb = tc // CONV_HALO
    n_h = s // CONV_HALO
    return pl.pallas_call(
        _conv_kernel,
        grid=(bsz, n_t),
        in_specs=[pl.BlockSpec((1, CONV_HALO, ch), lambda b, i: (b, jnp.maximum(i * hb - 1, 0), 0)),
                  pl.BlockSpec((1, tc, ch), lambda b, i: (b, i, 0)),
                  pl.BlockSpec((1, CONV_HALO, ch), lambda b, i: (b, jnp.minimum((i + 1) * hb, n_h - 1), 0)),
                  pl.BlockSpec((8, ch), lambda b, i: (0, 0)),
                  pl.BlockSpec((1, ch), lambda b, i: (0, 0))],
        out_specs=pl.BlockSpec((1, tc, ch), lambda b, i: (b, i, 0)),
        out_shape=jax.ShapeDtypeStruct((bsz, s, ch), BF16),
        scratch_shapes=[pltpu.VMEM((tc + 2 * CONV_HALO, ch), F32)],
        compiler_params=_cparams(("arbitrary", "arbitrary")),
        name="mixer_c_conv",
    )(xbc, xbc, xbc, conv_w, conv_b)


def _split3(x):
    hi = x.astype(BF16)
    r1 = x - hi.astype(F32)
    mid = r1.astype(BF16)
    lo = (r1 - mid.astype(F32)).astype(BF16)
    return hi, mid, lo


def _ssd_direction(x_ref, dt_ref, prm_ref, state_ref, y_ref, *, reverse):
    L = C_CHUNK
    col0 = C_HEADS if reverse else 0
    r = lax.broadcasted_iota(jnp.int32, (L, L), 0)
    c = lax.broadcasted_iota(jnp.int32, (L, L), 1)
    causal = (r <= c) if reverse else (r >= c)
    tri = jnp.where(causal, 1.0, 0.0).astype(BF16)

    dt_bias = prm_ref[0:1, :]
    a_neg = -jnp.exp(prm_ref[1:2, :])
    dtv = jax.nn.softplus(dt_ref[0] + dt_bias)
    a = dtv * a_neg
    hi, mid, lo = _split3(a)
    cs = (jnp.dot(tri, hi, preferred_element_type=F32) + jnp.dot(tri, mid, preferred_element_type=F32)
          + jnp.dot(tri, lo, preferred_element_type=F32))
    cs_t = cs.T
    tot = cs[0:1, :] if reverse else cs[L - 1:L, :]
    decay_in = jnp.exp(cs)
    decay_out = jnp.exp(tot - cs)
    chunk_decay = jnp.exp(tot)

    lane_head = lax.broadcasted_iota(jnp.int32, (L, 4 * C_HEAD_DIM), 1) // C_HEAD_DIM

    def expand(m, g):
        out = jnp.zeros((m.shape[0], 4 * C_HEAD_DIM), F32)
        for j in range(4):
            cj = col0 + g * 4 + j
            out = jnp.where(lane_head[:m.shape[0]] == j, m[:, cj:cj + 1], out)
        return out

    for g in range(C_GROUPS):
        xg = x_ref[0, :, g * 256:(g + 1) * 256].astype(F32)
        bm = x_ref[0, :, C_INNER + g * C_STATE:C_INNER + (g + 1) * C_STATE]
        cm = x_ref[0, :, C_INNER + C_BC + g * C_STATE:C_INNER + C_BC + (g + 1) * C_STATE]
        cb = lax.dot_general(cm, bm, (((1,), (1,)), ((), ())), preferred_element_type=F32)
        xdt = xg * expand(dtv, g)
        state = state_ref[g]
        y = jnp.dot(cm, state.astype(BF16), preferred_element_type=F32) * expand(decay_in, g)
        xdt_b = xdt.astype(BF16)
        parts = []
        for j in range(4):
            cj = col0 + g * 4 + j
            diff = cs[:, cj:cj + 1] - cs_t[cj:cj + 1, :]
            lm = jnp.exp(jnp.where(causal, diff, NEG_INF))
            gmat = (cb * lm).astype(BF16)
            parts.append(jnp.dot(gmat, xdt_b[:, j * C_HEAD_DIM:(j + 1) * C_HEAD_DIM],
                                 preferred_element_type=F32))
        y = y + jnp.concatenate(parts, axis=1)
        if not reverse:
            y = y + xg * expand(prm_ref[2:3, :], g)
        y_ref[0, :, g * 256:(g + 1) * 256] = y
        xs = (xdt * expand(decay_out, g)).astype(BF16)
        upd = lax.dot_general(bm, xs, (((0,), (0,)), ((), ())), preferred_element_type=F32)
        state_ref[g] = state * expand(chunk_decay, g) + upd


def _ssd_kernel(xf_ref, xb_ref, dtf_ref, dtb_ref, prm_ref, yf_ref, yb_ref, sf_ref, sb_ref):
    @pl.when(pl.program_id(1) == 0)
    def _():
        sf_ref[...] = jnp.zeros_like(sf_ref)
        sb_ref[...] = jnp.zeros_like(sb_ref)

    _ssd_direction(xf_ref, dtf_ref, prm_ref, sf_ref, yf_ref, reverse=False)
    _ssd_direction(xb_ref, dtb_ref, prm_ref, sb_ref, yb_ref, reverse=True)


def _ssd(xact, dt, prm):
    bsz, s, ch = xact.shape
    n_c = s // C_CHUNK
    fwd = lambda w: pl.BlockSpec((1, C_CHUNK, w), lambda b, i: (b, i, 0))
    bwd = lambda w: pl.BlockSpec((1, C_CHUNK, w), lambda b, i: (b, n_c - 1 - i, 0))
    return pl.pallas_call(
        _ssd_kernel,
        grid=(bsz, n_c),
        in_specs=[fwd(ch), bwd(ch), fwd(LANES), bwd(LANES), pl.BlockSpec((8, LANES), lambda b, i: (0, 0))],
        out_specs=[fwd(C_INNER), bwd(C_INNER)],
        out_shape=[jax.ShapeDtypeStruct((bsz, s, C_INNER), F32)] * 2,
        scratch_shapes=[pltpu.VMEM((C_GROUPS, C_STATE, 4 * C_HEAD_DIM), F32)] * 2,
        compiler_params=_cparams(("arbitrary", "arbitrary")),
        name="mixer_c_ssd",
    )(xact, xact, dt, dt, prm)


def _merge_kernel(x_ref, gate_ref, oa_ref, ga_ref, ob_ref, lse_ref, gb_ref, yf_ref, yb_ref, zc_ref,
                  mg_ref, bg_ref, nw_ref, wa_ref, wb_ref, wc_ref, wo_ref, o_ref):
    ya = (oa_ref[0].astype(F32) * _silu(ga_ref[0].astype(F32))).astype(BF16)
    l0, l1, l2 = lse_ref[0, 0], lse_ref[1, 0], lse_ref[2, 0]
    mx = jnp.maximum(jnp.maximum(l0, l1), l2)
    e0, e1, e2 = jnp.exp(l0 - mx), jnp.exp(l1 - mx), jnp.exp(l2 - mx)
    den = e0 + e1 + e2
    yb = (e0 / den) * ob_ref[0, 0] + (e1 / den) * ob_ref[1, 0] + (e2 / den) * ob_ref[2, 0]
    yb = (yb * _silu(gb_ref[0].astype(F32))).astype(BF16)
    y = (yf_ref[0] + yb_ref[0]) * _silu(zc_ref[0].astype(F32))
    yc = ((y * lax.rsqrt(jnp.mean(y * y, axis=-1, keepdims=True) + EPS)) * nw_ref[...]).astype(BF16)

    d = D_MODEL
    gates = lambda k: jax.nn.sigmoid(mg_ref[0, :, k * d:(k + 1) * d].astype(F32) + bg_ref[:, k * d:(k + 1) * d])
    merged = gates(0) * jnp.dot(ya, wa_ref[...], preferred_element_type=F32)
    merged = merged + gates(1) * jnp.dot(yb, wb_ref[...], preferred_element_type=F32)
    merged = merged + gates(2) * jnp.dot(yc, wc_ref[...], preferred_element_type=F32)
    out = jnp.dot(merged.astype(BF16), wo_ref[...], preferred_element_type=F32)
    o_ref[0] = x_ref[0] + gate_ref[0] * out


def _merge(x, gate, oa, ga, ob, lse, gb, yf, yb, zc, mg, b_gate, ssm_w, wa, wb, wc, wo, tm=256):
    bsz, s, d = x.shape
    row = lambda w: pl.BlockSpec((1, tm, w), lambda b, i: (b, i, 0))
    grp = pl.BlockSpec((B_GROUPS, 1, tm, B_OUT), lambda b, i: (0, b, i, 0))
    const2 = lambda shp: pl.BlockSpec(shp, lambda b, i: (0, 0))
    return pl.pallas_call(
        _merge_kernel,
        grid=(bsz, s // tm),
        in_specs=[row(d), pl.BlockSpec((1, 1, d), lambda b, i: (b, 0, 0)),
                  row(A_Q), row(A_Q), grp, grp, row(B_OUT), row(C_INNER), row(C_INNER), row(C_INNER),
                  row(N_BRANCH * d), const2((1, N_BRANCH * d)), const2((1, C_INNER)),
                  const2(wa.shape), const2(wb.shape), const2(wc.shape), const2(wo.shape)],
        out_specs=row(d),
        out_shape=jax.ShapeDtypeStruct((bsz, s, d), F32),
        compiler_params=_cparams(("arbitrary", "arbitrary")),
        name="merge_out",
    )(x, gate, oa, ga, ob, lse, gb, yf, yb, zc, mg, b_gate, ssm_w, wa, wb, wc, wo)


def _rope_tables(s):
    quarter = HEAD_DIM // 4
    freqs = ROPE_THETA ** (-jnp.arange(quarter, dtype=F32) / quarter)
    t = jnp.arange(s)
    ang_r = (t // GRID_W).astype(F32)[:, None] * freqs
    ang_c = (t % GRID_W).astype(F32)[:, None] * freqs
    cos = jnp.concatenate([jnp.cos(ang_r)] * 2 + [jnp.cos(ang_c)] * 2, axis=-1)
    sin = jnp.concatenate([-jnp.sin(ang_r), jnp.sin(ang_r), -jnp.sin(ang_c), jnp.sin(ang_c)], axis=-1)
    return jnp.tile(cos, (1, LANES // HEAD_DIM)), jnp.tile(sin, (1, LANES // HEAD_DIM))


def _layer_weights(w_in_l):
    cols = lambda k: w_in_l[:, SPLIT_OFFSETS[k]:SPLIT_OFFSETS[k + 1]]
    d = w_in_l.shape[0]
    perm = lambda w: w.reshape(d, A_HEADS, HEAD_DIM)[:, A_HEAD_ORDER, :].reshape(d, A_Q)
    dt_cols = jnp.concatenate([cols(12), cols(13), jnp.zeros((d, LANES - 2 * C_HEADS), w_in_l.dtype)], axis=1)
    pieces = {"qa": perm(cols(0)), "ka": cols(1), "ga": perm(cols(3)), "qb": cols(4), "kb": cols(5),
              "vb": cols(6), "gb": cols(7), "xbc": jnp.concatenate([cols(8), cols(10), cols(11)], axis=1),
              "zc": cols(9), "dt": dt_cols, "mg": cols(14)}
    w_main = jnp.concatenate([pieces[n] for n, _ in _W_GROUPS], axis=1).astype(BF16)
    w_t = cols(2).T.astype(BF16)
    return w_main, w_t


def _pad_row(v, width=LANES):
    return jnp.pad(v.astype(F32), (0, width - v.shape[0]))


def kernel(x, c, norm_w, w_ada, b_ada, w_in, b_gate, q_norm_a, k_norm_a, q_norm_b, k_norm_b, rel_bias,
           conv_w, conv_b, a_log, dt_bias, d_skip, ssm_norm_w, w_proj_a, w_proj_b, w_proj_c, w_out):
    bsz, s, d = x.shape
    depth = w_in.shape[0]
    c_pad = jnp.pad(c, ((0, 8 - bsz), (0, 0)))
    mod = _modulation(c_pad, w_ada, b_ada)[:, :bsz]
    cos_t, sin_t = _rope_tables(s)
    bias_tbl = _band_bias(rel_bias)
    tile2 = lambda v: jnp.tile(v.astype(F32), LANES // HEAD_DIM)
    for l in range(depth):
        shift, scale, gate = (mod[l, :, k * d:(k + 1) * d].reshape(bsz, 1, d) for k in range(3))
        w_main, w_t = _layer_weights(w_in[l])
        head_w = jnp.stack([tile2(q_norm_a[l]), tile2(k_norm_a[l]), tile2(q_norm_b[l]), tile2(k_norm_b[l])]
                           + [jnp.zeros((LANES,), F32)] * 4, axis=0)
        (qa, ka, vt, ga, qb, kb, vb, gb, xbc, zc, dt, mg) = _in_projection(
            x, 1.0 + scale, shift, norm_w[l].reshape(1, d), cos_t, sin_t, head_w, w_main, w_t)
        oa = _mixer_a(qa, ka, vt)
        ob, lse = zip(*[_mixer_b_group(qb, kb, vb, bias_tbl[g], g) for g in range(B_GROUPS)])
        conv_w8 = jnp.pad(conv_w[l], ((0, 8 - C_CONV), (0, 0)))
        xact = _conv_silu(xbc, conv_w8, conv_b[l].reshape(1, C_XBC))
        prm = jnp.stack([_pad_row(dt_bias[l].reshape(-1)), _pad_row(a_log[l].reshape(-1)),
                         _pad_row(jnp.tile(d_skip[l], 2))] + [jnp.zeros((LANES,), F32)] * 5, axis=0)
        yf, yb = _ssd(xact, dt, prm)
        perm_rows = lambda w: w.reshape(A_HEADS, HEAD_DIM, d)[A_HEAD_ORDER, :, :].reshape(A_Q, d)
        x = _merge(x, gate, oa, ga, jnp.stack(ob), jnp.stack(lse), gb, yf, yb, zc, mg,
                   b_gate[l].reshape(1, -1), ssm_norm_w[l].reshape(1, -1),
                   perm_rows(w_proj_a[l]).astype(BF16), w_proj_b[l].astype(BF16),
                   w_proj_c[l].astype(BF16), w_out[l].astype(BF16))
    return x
```

```python
import functools
import math

import numpy as np
import jax
import jax.numpy as jnp
from jax import lax
from jax.experimental import pallas as pl
from jax.experimental.pallas import tpu as pltpu

F32 = jnp.float32
BF16 = jnp.bfloat16

D_MODEL = 1024
HEAD_DIM = 64
EPS = 1e-6
NEG_INF = -1e30
GRID_W = 64
A_HEADS = 8
A_KV_HEADS = 2
ROPE_THETA = 10000.0
B_GROUPS = 3
B_HEADS_PER_GROUP = 4
B_WINDOWS = (128, 512, 2048)
B_DILATIONS = (1, 4, 16)
N_BUCKETS = 32
MAX_DISTANCE = 1024
C_HEADS = 8
C_HEAD_DIM = 64
C_INNER = C_HEADS * C_HEAD_DIM
C_GROUPS = 2
C_STATE = 128
C_CONV = 5
C_CHUNK = 128
N_BRANCH = 3

A_Q = A_HEADS * HEAD_DIM
A_KV = A_KV_HEADS * HEAD_DIM
B_QKV = B_GROUPS * B_HEADS_PER_GROUP * HEAD_DIM
B_OUT = B_HEADS_PER_GROUP * HEAD_DIM
C_BC = C_GROUPS * C_STATE
C_XBC = C_INNER + 2 * C_BC
SPLIT_WIDTHS = (A_Q, A_KV, A_KV, A_Q, B_QKV, B_QKV, B_QKV, B_OUT,
                C_INNER, C_INNER, C_BC, C_BC, C_HEADS, C_HEADS, N_BRANCH * D_MODEL)
SPLIT_OFFSETS = tuple(int(v) for v in np.cumsum((0,) + SPLIT_WIDTHS))

LANES = 128
VMEM_LIMIT = 56 * 1024 * 1024
LOG2E = math.log2(math.e)

_W_GROUPS = (("qa", A_Q), ("ka", A_KV), ("ga", A_Q), ("qb", B_QKV), ("kb", B_QKV), ("vb", B_QKV),
             ("gb", B_OUT), ("xbc", C_XBC), ("zc", C_INNER), ("dt", LANES), ("mg", N_BRANCH * D_MODEL))
_W_OFF = {}
_o = 0
for _n, _w in _W_GROUPS:
    _W_OFF[_n] = (_o, _w)
    _o += _w
W_MAIN_COLS = _o

A_HEAD_ORDER = tuple(h for j in range(4) for h in (j, j + 4))


def _cparams(sem):
    return pltpu.CompilerParams(dimension_semantics=sem, vmem_limit_bytes=VMEM_LIMIT)


def _silu(x):
    return x * jax.nn.sigmoid(x)


def _mod_kernel(c_ref, w_ref, b_ref, o_ref):
    c = c_ref[...]
    o_ref[0] = jnp.dot(_silu(c), w_ref[0], preferred_element_type=F32,
                       precision=lax.Precision.HIGHEST) + b_ref[0]


def _modulation(c_pad, w_ada, b_ada):
    depth, d, n = w_ada.shape
    tn = 512
    return pl.pallas_call(
        _mod_kernel,
        grid=(depth, n // tn),
        in_specs=[pl.BlockSpec((c_pad.shape[0], d), lambda l, j: (0, 0)),
                  pl.BlockSpec((1, d, tn), lambda l, j: (l, 0, j)),
                  pl.BlockSpec((1, 1, tn), lambda l, j: (l, 0, j))],
        out_specs=pl.BlockSpec((1, c_pad.shape[0], tn), lambda l, j: (l, 0, j)),
        out_shape=jax.ShapeDtypeStruct((depth, c_pad.shape[0], n), F32),
        compiler_params=_cparams(("arbitrary", "arbitrary")),
        name="adaln_mod",
    )(c_pad, w_ada, b_ada.reshape(depth, 1, n))


def _head_sumsq_matrix():
    r = lax.broadcasted_iota(jnp.int32, (LANES, LANES), 0) // HEAD_DIM
    c = lax.broadcasted_iota(jnp.int32, (LANES, LANES), 1) // HEAD_DIM
    return jnp.where(r == c, 1.0 / HEAD_DIM, 0.0).astype(BF16)


def _inproj_kernel(x_ref, sc_ref, sh_ref, nw_ref, cos_ref, sin_ref, hw_ref, w_ref, wt_ref,
                   qa_ref, ka_ref, vt_ref, ga_ref, qb_ref, kb_ref, vb_ref, gb_ref,
                   xbc_ref, zc_ref, dt_ref, mg_ref):
    x = x_ref[0]
    ms = jnp.mean(x * x, axis=-1, keepdims=True)
    h = (x * lax.rsqrt(ms + EPS)) * nw_ref[...]
    h = h * sc_ref[0] + sh_ref[0]
    hb = h.astype(BF16)

    bd = _head_sumsq_matrix()
    lane = lax.broadcasted_iota(jnp.int32, (1, LANES), 1)
    first_half = (lane % 32) < 16
    cos = cos_ref[...]
    sin = sin_ref[...]

    def proj(name, c0, c1):
        off, _ = _W_OFF[name]
        return jnp.dot(hb, w_ref[:, off + c0:off + c1], preferred_element_type=F32)

    def head_norm(y, w_row):
        ss = jnp.dot((y * y).astype(BF16), bd, preferred_element_type=F32)
        return (y * lax.rsqrt(ss + EPS)) * w_row

    def rope(y):
        sw = jnp.where(first_half, pltpu.roll(y, LANES - 16, 1), pltpu.roll(y, 16, 1))
        return y * cos + sw * sin

    for j in range(A_Q // LANES):
        y = rope(head_norm(proj("qa", j * LANES, (j + 1) * LANES), hw_ref[0:1, :]))
        qa_ref[0, :, j * LANES:(j + 1) * LANES] = (y * (HEAD_DIM ** -0.5 * LOG2E)).astype(BF16)
    ka_ref[0] = rope(head_norm(proj("ka", 0, LANES), hw_ref[1:2, :])).astype(BF16)
    vt_ref[0] = lax.dot_general(wt_ref[...], hb, (((1,), (1,)), ((), ())),
                                preferred_element_type=F32).astype(BF16)
    for j in range(0, A_Q, 256):
        ga_ref[0, :, j:j + 256] = proj("ga", j, j + 256).astype(BF16)
    for j in range(B_QKV // LANES):
        sl = slice(j * LANES, (j + 1) * LANES)
        qb_ref[0, :, sl] = (head_norm(proj("qb", sl.start, sl.stop), hw_ref[2:3, :])
                            * (HEAD_DIM ** -0.5)).astype(BF16)
        kb_ref[0, :, sl] = head_norm(proj("kb", sl.start, sl.stop), hw_ref[3:4, :]).astype(BF16)
    for j in range(0, B_QKV, 256):
        vb_ref[0, :, j:j + 256] = proj("vb", j, j + 256).astype(BF16)
    gb_ref[0] = proj("gb", 0, B_OUT).astype(BF16)
    for j in range(0, C_XBC, 256):
        xbc_ref[0, :, j:j + 256] = proj("xbc", j, j + 256).astype(BF16)
    for j in range(0, C_INNER, 256):
        zc_ref[0, :, j:j + 256] = proj("zc", j, j + 256).astype(BF16)
    dt_ref[0] = proj("dt", 0, LANES)
    for j in range(0, N_BRANCH * D_MODEL, 512):
        mg_ref[0, :, j:j + 512] = proj("mg", j, j + 512).astype(BF16)


def _in_projection(x, sc1p, shift, norm_w, cos_t, sin_t, head_w, w_main, w_t, tm=256):
    bsz, s, d = x.shape
    nt = s // tm
    row = lambda i, w: pl.BlockSpec((1, tm, w), lambda b, i: (b, i, 0))
    const2 = lambda shp: pl.BlockSpec(shp, lambda b, i: (0, 0))
    outs = [("qa", A_Q, BF16), ("ka", A_KV, BF16), None, ("ga", A_Q, BF16), ("qb", B_QKV, BF16),
            ("kb", B_QKV, BF16), ("vb", B_QKV, BF16), ("gb", B_OUT, BF16), ("xbc", C_XBC, BF16),
            ("zc", C_INNER, BF16), ("dt", LANES, F32), ("mg", N_BRANCH * D_MODEL, BF16)]
    out_shape, out_specs = [], []
    for o in outs:
        if o is None:
            out_shape.append(jax.ShapeDtypeStruct((bsz, A_KV, s), BF16))
            out_specs.append(pl.BlockSpec((1, A_KV, tm), lambda b, i: (b, 0, i)))
        else:
            out_shape.append(jax.ShapeDtypeStruct((bsz, s, o[1]), o[2]))
            out_specs.append(row(0, o[1]))
    return pl.pallas_call(
        _inproj_kernel,
        grid=(bsz, nt),
        in_specs=[row(0, d),
                  pl.BlockSpec((1, 1, d), lambda b, i: (b, 0, 0)),
                  pl.BlockSpec((1, 1, d), lambda b, i: (b, 0, 0)),
                  const2((1, d)),
                  pl.BlockSpec((tm, LANES), lambda b, i: (i, 0)),
                  pl.BlockSpec((tm, LANES), lambda b, i: (i, 0)),
                  const2((8, LANES)),
                  const2(w_main.shape),
                  const2(w_t.shape)],
        out_specs=out_specs,
        out_shape=out_shape,
        compiler_params=_cparams(("arbitrary", "arbitrary")),
        name="norm_inproj",
    )(x, sc1p, shift, norm_w, cos_t, sin_t, head_w, w_main, w_t)


A_AUG = HEAD_DIM + 8


def _attn_a_kernel(q_ref, k_ref, vt_ref, o_ref, qt_ref, m_ref, acc_ref, st_ref, *, kc):
    tq = q_ref.shape[1]
    s_len = k_ref.shape[1]
    n_blk = A_Q // LANES
    lane = lax.broadcasted_iota(jnp.int32, (tq, LANES), 1)
    for j in range(n_blk):
        qblk = q_ref[0, :, j * LANES:(j + 1) * LANES]
        for kv in range(A_KV_HEADS):
            in_half = (lane < HEAD_DIM) if kv == 0 else (lane >= HEAD_DIM)
            qt_ref[A_KV_HEADS * j + kv] = jnp.where(in_half, qblk, jnp.zeros_like(qblk))
    m_ref[...] = jnp.full(m_ref.shape, NEG_INF, F32)
    acc_ref[...] = jnp.zeros(acc_ref.shape, F32)
    ones = jnp.ones((A_AUG - HEAD_DIM, kc), BF16)

    n_chunks = s_len // kc

    def scores(c, h):
        off = pl.multiple_of(c * kc, kc)
        return lax.dot_general(k_ref[0, pl.ds(off, kc), :], qt_ref[h], (((1,), (1,)), ((), ())),
                               preferred_element_type=F32)

    st_ref[...] = scores(0, 0)

    def body(c, carry):
        off = pl.multiple_of(c * kc, kc)
        vaug = [jnp.concatenate([vt_ref[0, kv * HEAD_DIM:(kv + 1) * HEAD_DIM, pl.ds(off, kc)], ones], axis=0)
                for kv in range(A_KV_HEADS)]
        st = st_ref[...]
        for h in range(A_HEADS):
            if h + 1 < A_HEADS:
                st_next = scores(c, h + 1)
            else:
                st_next = scores(jnp.minimum(c + 1, n_chunks - 1), 0)
            m = m_ref[h:h + 1, :]
            m_new = jnp.maximum(m, jnp.max(st, axis=0, keepdims=True))
            alpha = jnp.exp2(m - m_new)
            p = jnp.exp2(st - m_new).astype(BF16)
            acc_ref[h] = alpha * acc_ref[h] + jnp.dot(vaug[h % A_KV_HEADS], p, preferred_element_type=F32)
            m_ref[h:h + 1, :] = m_new
            st = st_next
        st_ref[...] = st
        return carry

    lax.fori_loop(0, n_chunks, body, 0)
    for j in range(n_blk):
        halves = []
        for kv in range(A_KV_HEADS):
            a = acc_ref[A_KV_HEADS * j + kv]
            halves.append(a[:HEAD_DIM] / a[HEAD_DIM:HEAD_DIM + 1])
        o_ref[0, :, j * LANES:(j + 1) * LANES] = jnp.concatenate(halves, axis=0).T.astype(o_ref.dtype)


def _mixer_a(qa, ka, vt, tq=256, kc=512):
    bsz, s, _ = qa.shape
    return pl.pallas_call(
        functools.partial(_attn_a_kernel, kc=kc),
        grid=(bsz, s // tq),
        in_specs=[pl.BlockSpec((1, tq, A_Q), lambda b, i: (b, i, 0)),
                  pl.BlockSpec((1, s, A_KV), lambda b, i: (b, 0, 0)),
                  pl.BlockSpec((1, A_KV, s), lambda b, i: (b, 0, 0))],
        out_specs=pl.BlockSpec((1, tq, A_Q), lambda b, i: (b, i, 0)),
        out_shape=jax.ShapeDtypeStruct((bsz, s, A_Q), BF16),
        scratch_shapes=[pltpu.VMEM((A_HEADS, tq, LANES), BF16),
                        pltpu.VMEM((A_HEADS, tq), F32),
                        pltpu.VMEM((A_HEADS, A_AUG, tq), F32),
                        pltpu.VMEM((kc, tq), F32)],
        compiler_params=_cparams(("arbitrary", "arbitrary")),
        name="mixer_a_attn",
    )(qa, ka, vt)


B_SIDE = 64
B_TQ = 256
B_BAND = B_TQ + 2 * B_SIDE


def _attn_b_kernel(q_ref, kp_ref, kc_ref, kn_ref, vp_ref, vc_ref, vn_ref, bias_ref, o_ref, lse_ref):
    mi = pl.program_id(2)
    n_m = pl.num_programs(2)
    q = q_ref[0]
    kband = jnp.concatenate([kp_ref[0], kc_ref[0], kn_ref[0]], axis=0)
    vband = jnp.concatenate([vp_ref[0], vc_ref[0], vn_ref[0]], axis=0)
    kk = lax.broadcasted_iota(jnp.int32, (1, B_BAND), 1)
    key_ok = jnp.logical_and(jnp.logical_or(kk >= B_SIDE, mi > 0),
                             jnp.logical_or(kk < B_SIDE + B_TQ, mi < n_m - 1))
    lane = lax.broadcasted_iota(jnp.int32, (B_TQ, B_OUT), 1) // HEAD_DIM
    o_acc = jnp.zeros((B_TQ, B_OUT), F32)
    lse_acc = jnp.zeros((B_TQ, B_OUT), F32)
    for h in range(B_HEADS_PER_GROUP):
        qh = jnp.where(lane == h, q, jnp.zeros_like(q))
        s = lax.dot_general(qh, kband, (((1,), (1,)), ((), ())), preferred_element_type=F32)
        s = jnp.where(key_ok, s + bias_ref[h], NEG_INF)
        mx = jnp.max(s, axis=-1, keepdims=True)
        e = jnp.exp(s - mx)
        den = jnp.sum(e, axis=-1, keepdims=True)
        p = (e / den).astype(BF16)
        oh = jnp.dot(p, vband, preferred_element_type=F32)
        o_acc = jnp.where(lane == h, oh, o_acc)
        lse_acc = jnp.where(lane == h, mx + jnp.log(den), lse_acc)
    o_ref[0] = o_acc.astype(o_ref.dtype)
    lse_ref[0] = lse_acc


def _mixer_b_group(qb, kb, vb, bias_g, g):
    bsz, s, _ = qb.shape
    d = B_DILATIONS[g]
    m = s // d
    n_m = m // B_TQ
    hb = B_TQ // B_SIDE
    n_side = m // B_SIDE
    view = lambda t: t.reshape(bsz, m, d * B_QKV)
    col = lambda r: r * B_GROUPS + g
    cur = pl.BlockSpec((1, B_TQ, B_OUT), lambda b, r, i: (b, i, col(r)))
    prev = pl.BlockSpec((1, B_SIDE, B_OUT), lambda b, r, i: (b, jnp.maximum(i * hb - 1, 0), col(r)))
    nxt = pl.BlockSpec((1, B_SIDE, B_OUT),
                       lambda b, r, i: (b, jnp.minimum((i + 1) * hb, n_side - 1), col(r)))
    out_spec = pl.BlockSpec((1, B_TQ, B_OUT), lambda b, r, i: (b, i, r))
    o, lse = pl.pallas_call(
        _attn_b_kernel,
        grid=(bsz, d, n_m),
        in_specs=[cur, prev, cur, nxt, prev, cur, nxt,
                  pl.BlockSpec((B_HEADS_PER_GROUP, B_TQ, B_BAND), lambda b, r, i: (0, 0, 0))],
        out_specs=[out_spec, out_spec],
        out_shape=[jax.ShapeDtypeStruct((bsz, m, d * B_OUT), F32),
                   jax.ShapeDtypeStruct((bsz, m, d * B_OUT), F32)],
        compiler_params=_cparams(("arbitrary", "arbitrary", "arbitrary")),
        name=f"mixer_b_attn_g{g}",
    )(view(qb), view(kb), view(kb), view(kb), view(vb), view(vb), view(vb), bias_g)
    return o.reshape(bsz, s, B_OUT), lse.reshape(bsz, s, B_OUT)


def _t5_bucket(rel):
    nb = N_BUCKETS // 2
    max_exact = nb // 2
    ret = jnp.where(rel > 0, nb, 0)
    n = jnp.abs(rel)
    nf = jnp.maximum(n, 1).astype(F32)
    large = max_exact + (jnp.log(nf / max_exact) / math.log(MAX_DISTANCE / max_exact)
                         * (nb - max_exact)).astype(jnp.int32)
    large = jnp.minimum(large, nb - 1)
    return ret + jnp.where(n < max_exact, n, large)


def _band_bias(rel_bias):
    qi = jnp.arange(B_TQ)[:, None]
    kk = jnp.arange(B_BAND)[None, :]
    rel = kk - B_SIDE - qi
    in_band = jnp.abs(rel) <= B_SIDE
    tables = []
    for g in range(B_GROUPS):
        tbl = rel_bias[:, g * B_HEADS_PER_GROUP:(g + 1) * B_HEADS_PER_GROUP]
        bias = tbl[_t5_bucket(rel * B_DILATIONS[g])].transpose(2, 0, 1).astype(F32)
        tables.append(jnp.where(in_band[None], bias, NEG_INF))
    return jnp.stack(tables, axis=0)


CONV_HALO = 8


def _conv_kernel(xp_ref, xc_ref, xn_ref, w_ref, b_ref, o_ref, ext_ref):
    i = pl.program_id(1)
    n_t = pl.num_programs(1)
    tc = xc_ref.shape[1]
    prev = jnp.where(i > 0, xp_ref[0].astype(F32), 0.0)
    nxt = jnp.where(i < n_t - 1, xn_ref[0].astype(F32), 0.0)
    ext_ref[0:CONV_HALO, :] = prev
    ext_ref[CONV_HALO:CONV_HALO + tc, :] = xc_ref[0].astype(F32)
    ext_ref[CONV_HALO + tc:, :] = nxt
    acc = jnp.zeros((tc, xc_ref.shape[2]), F32) + b_ref[...]
    for k in range(C_CONV):
        start = CONV_HALO + k - C_CONV // 2
        acc = acc + ext_ref[start:start + tc, :] * w_ref[k:k + 1, :]
    o_ref[0] = _silu(acc).astype(o_ref.dtype)


def _conv_silu(xbc, conv_w, conv_b, tc=512):
    bsz, s, ch = xbc.shape
    n_t = s // tc
    hb = tc // CONV_HALO
    n_h = s // CONV_HALO
    return pl.pallas_call(
        _conv_kernel,
        grid=(bsz, n_t),
        in_specs=[pl.BlockSpec((1, CONV_HALO, ch), lambda b, i: (b, jnp.maximum(i * hb - 1, 0), 0)),
                  pl.BlockSpec((1, tc, ch), lambda b, i: (b, i, 0)),
                  pl.BlockSpec((1, CONV_HALO, ch), lambda b, i: (b, jnp.minimum((i + 1) * hb, n_h - 1), 0)),
                  pl.BlockSpec((8, ch), lambda b, i: (0, 0)),
                  pl.BlockSpec((1, ch), lambda b, i: (0, 0))],
        out_specs=pl.BlockSpec((1, tc, ch), lambda b, i: (b, i, 0)),
        out_shape=jax.ShapeDtypeStruct((bsz, s, ch), BF16),
        scratch_shapes=[pltpu.VMEM((tc + 2 * CONV_HALO, ch), F32)],
        compiler_params=_cparams(("arbitrary", "arbitrary")),
        name="mixer_c_conv",
    )(xbc, xbc, xbc, conv_w, conv_b)


def _split3(x):
    hi = x.astype(BF16)
    r1 = x - hi.astype(F32)
    mid = r1.astype(BF16)
    lo = (r1 - mid.astype(F32)).astype(BF16)
    return hi, mid, lo


def _ssd_direction(x_ref, dt_ref, prm_ref, state_ref, y_ref, *, reverse):
    L = C_CHUNK
    col0 = C_HEADS if reverse else 0
    r = lax.broadcasted_iota(jnp.int32, (L, L), 0)
    c = lax.broadcasted_iota(jnp.int32, (L, L), 1)
    causal = (r <= c) if reverse else (r >= c)
    tri = jnp.where(causal, 1.0, 0.0).astype(BF16)

    dt_bias = prm_ref[0:1, :]
    a_neg = -jnp.exp(prm_ref[1:2, :])
    dtv = jax.nn.softplus(dt_ref[0] + dt_bias)
    a = dtv * a_neg
    hi, mid, lo = _split3(a)
    cs = (jnp.dot(tri, hi, preferred_element_type=F32) + jnp.dot(tri, mid, preferred_element_type=F32)
          + jnp.dot(tri, lo, preferred_element_type=F32))
    cs_t = cs.T
    tot = cs[0:1, :] if reverse else cs[L - 1:L, :]
    decay_in = jnp.exp(cs)
    decay_out = jnp.exp(tot - cs)
    chunk_decay = jnp.exp(tot)

    lane_head = lax.broadcasted_iota(jnp.int32, (L, 4 * C_HEAD_DIM), 1) // C_HEAD_DIM

    def expand(m, g):
        out = jnp.zeros((m.shape[0], 4 * C_HEAD_DIM), F32)
        for j in range(4):
            cj = col0 + g * 4 + j
            out = jnp.where(lane_head[:m.shape[0]] == j, m[:, cj:cj + 1], out)
        return out

    for g in range(C_GROUPS):
        xg = x_ref[0, :, g * 256:(g + 1) * 256].astype(F32)
        bm = x_ref[0, :, C_INNER + g * C_STATE:C_INNER + (g + 1) * C_STATE]
        cm = x_ref[0, :, C_INNER + C_BC + g * C_STATE:C_INNER + C_BC + (g + 1) * C_STATE]
        cb = lax.dot_general(cm, bm, (((1,), (1,)), ((), ())), preferred_element_type=F32)
        xdt = xg * expand(dtv, g)
        state = state_ref[g]
        y = jnp.dot(cm, state.astype(BF16), preferred_element_type=F32) * expand(decay_in, g)
        xdt_b = xdt.astype(BF16)
        parts = []
        for j in range(4):
            cj = col0 + g * 4 + j
            diff = cs[:, cj:cj + 1] - cs_t[cj:cj + 1, :]
            lm = jnp.exp(jnp.where(causal, diff, NEG_INF))
            gmat = (cb * lm).astype(BF16)
            parts.append(jnp.dot(gmat, xdt_b[:, j * C_HEAD_DIM:(j + 1) * C_HEAD_DIM],
                                 preferred_element_type=F32))
        y = y + jnp.concatenate(parts, axis=1)
        if not reverse:
            y = y + xg * expand(prm_ref[2:3, :], g)
        y_ref[0, :, g * 256:(g + 1) * 256] = y
        xs = (xdt * expand(decay_out, g)).astype(BF16)
        upd = lax.dot_general(bm, xs, (((0,), (0,)), ((), ())), preferred_element_type=F32)
        state_ref[g] = state * expand(chunk_decay, g) + upd


def _ssd_kernel(xf_ref, xb_ref, dtf_ref, dtb_ref, prm_ref, yf_ref, yb_ref, sf_ref, sb_ref):
    @pl.when(pl.program_id(1) == 0)
    def _():
        sf_ref[...] = jnp.zeros_like(sf_ref)
        sb_ref[...] = jnp.zeros_like(sb_ref)

    _ssd_direction(xf_ref, dtf_ref, prm_ref, sf_ref, yf_ref, reverse=False)
    _ssd_direction(xb_ref, dtb_ref, prm_ref, sb_ref, yb_ref, reverse=True)


def _ssd(xact, dt, prm):
    bsz, s, ch = xact.shape
    n_c = s // C_CHUNK
    fwd = lambda w: pl.BlockSpec((1, C_CHUNK, w), lambda b, i: (b, i, 0))
    bwd = lambda w: pl.BlockSpec((1, C_CHUNK, w), lambda b, i: (b, n_c - 1 - i, 0))
    return pl.pallas_call(
        _ssd_kernel,
        grid=(bsz, n_c),
        in_specs=[fwd(ch), bwd(ch), fwd(LANES), bwd(LANES), pl.BlockSpec((8, LANES), lambda b, i: (0, 0))],
        out_specs=[fwd(C_INNER), bwd(C_INNER)],
        out_shape=[jax.ShapeDtypeStruct((bsz, s, C_INNER), F32)] * 2,
        scratch_shapes=[pltpu.VMEM((C_GROUPS, C_STATE, 4 * C_HEAD_DIM), F32)] * 2,
        compiler_params=_cparams(("arbitrary", "arbitrary")),
        name="mixer_c_ssd",
    )(xact, xact, dt, dt, prm)


def _merge_kernel(x_ref, gate_ref, oa_ref, ga_ref, ob_ref, lse_ref, gb_ref, yf_ref, yb_ref, zc_ref,
                  mg_ref, bg_ref, nw_ref, wa_ref, wb_ref, wc_ref, wo_ref, o_ref):
    ya = (oa_ref[0].astype(F32) * _silu(ga_ref[0].astype(F32))).astype(BF16)
    l0, l1, l2 = lse_ref[0, 0], lse_ref[1, 0], lse_ref[2, 0]
    mx = jnp.maximum(jnp.maximum(l0, l1), l2)
    e0, e1, e2 = jnp.exp(l0 - mx), jnp.exp(l1 - mx), jnp.exp(l2 - mx)
    den = e0 + e1 + e2
    yb = (e0 / den) * ob_ref[0, 0] + (e1 / den) * ob_ref[1, 0] + (e2 / den) * ob_ref[2, 0]
    yb = (yb * _silu(gb_ref[0].astype(F32))).astype(BF16)
    y = (yf_ref[0] + yb_ref[0]) * _silu(zc_ref[0].astype(F32))
    yc = ((y * lax.rsqrt(jnp.mean(y * y, axis=-1, keepdims=True) + EPS)) * nw_ref[...]).astype(BF16)

    d = D_MODEL
    gates = lambda k: jax.nn.sigmoid(mg_ref[0, :, k * d:(k + 1) * d].astype(F32) + bg_ref[:, k * d:(k + 1) * d])
    merged = gates(0) * jnp.dot(ya, wa_ref[...], preferred_element_type=F32)
    merged = merged + gates(1) * jnp.dot(yb, wb_ref[...], preferred_element_type=F32)
    merged = merged + gates(2) * jnp.dot(yc, wc_ref[...], preferred_element_type=F32)
    out = jnp.dot(merged.astype(BF16), wo_ref[...], preferred_element_type=F32)
    o_ref[0] = x_ref[0] + gate_ref[0] * out


def _merge(x, gate, oa, ga, ob, lse, gb, yf, yb, zc, mg, b_gate, ssm_w, wa, wb, wc, wo, tm=256):
    bsz, s, d = x.shape
    row = lambda w: pl.BlockSpec((1, tm, w), lambda b, i: (b, i, 0))
    grp = pl.BlockSpec((B_GROUPS, 1, tm, B_OUT), lambda b, i: (0, b, i, 0))
    const2 = lambda shp: pl.BlockSpec(shp, lambda b, i: (0, 0))
    return pl.pallas_call(
        _merge_kernel,
        grid=(bsz, s // tm),
        in_specs=[row(d), pl.BlockSpec((1, 1, d), lambda b, i: (b, 0, 0)),
                  row(A_Q), row(A_Q), grp, grp, row(B_OUT), row(C_INNER), row(C_INNER), row(C_INNER),
                  row(N_BRANCH * d), const2((1, N_BRANCH * d)), const2((1, C_INNER)),
                  const2(wa.shape), const2(wb.shape), const2(wc.shape), const2(wo.shape)],
        out_specs=row(d),
        out_shape=jax.ShapeDtypeStruct((bsz, s, d), F32),
        compiler_params=_cparams(("arbitrary", "arbitrary")),
        name="merge_out",
    )(x, gate, oa, ga, ob, lse, gb, yf, yb, zc, mg, b_gate, ssm_w, wa, wb, wc, wo)


def _rope_tables(s):
    quarter = HEAD_DIM // 4
    freqs = ROPE_THETA ** (-jnp.arange(quarter, dtype=F32) / quarter)
    t = jnp.arange(s)
    ang_r = (t // GRID_W).astype(F32)[:, None] * freqs
    ang_c = (t % GRID_W).astype(F32)[:, None] * freqs
    cos = jnp.concatenate([jnp.cos(ang_r)] * 2 + [jnp.cos(ang_c)] * 2, axis=-1)
    sin = jnp.concatenate([-jnp.sin(ang_r), jnp.sin(ang_r), -jnp.sin(ang_c), jnp.sin(ang_c)], axis=-1)
    return jnp.tile(cos, (1, LANES // HEAD_DIM)), jnp.tile(sin, (1, LANES // HEAD_DIM))


def _layer_weights(w_in_l):
    cols = lambda k: w_in_l[:, SPLIT_OFFSETS[k]:SPLIT_OFFSETS[k + 1]]
    d = w_in_l.shape[0]
    perm = lambda w: w.reshape(d, A_HEADS, HEAD_DIM)[:, A_HEAD_ORDER, :].reshape(d, A_Q)
    dt_cols = jnp.concatenate([cols(12), cols(13), jnp.zeros((d, LANES - 2 * C_HEADS), w_in_l.dtype)], axis=1)
    pieces = {"qa": perm(cols(0)), "ka": cols(1), "ga": perm(cols(3)), "qb": cols(4), "kb": cols(5),
              "vb": cols(6), "gb": cols(7), "xbc": jnp.concatenate([cols(8), cols(10), cols(11)], axis=1),
              "zc": cols(9), "dt": dt_cols, "mg": cols(14)}
    w_main = jnp.concatenate([pieces[n] for n, _ in _W_GROUPS], axis=1).astype(BF16)
    w_t = cols(2).T.astype(BF16)
    return w_main, w_t


def _pad_row(v, width=LANES):
    return jnp.pad(v.astype(F32), (0, width - v.shape[0]))


def kernel(x, c, norm_w, w_ada, b_ada, w_in, b_gate, q_norm_a, k_norm_a, q_norm_b, k_norm_b, rel_bias,
           conv_w, conv_b, a_log, dt_bias, d_skip, ssm_norm_w, w_proj_a, w_proj_b, w_proj_c, w_out):
    bsz, s, d = x.shape
    depth = w_in.shape[0]
    c_pad = jnp.pad(c, ((0, 8 - bsz), (0, 0)))
    mod = _modulation(c_pad, w_ada, b_ada)[:, :bsz]
    cos_t, sin_t = _rope_tables(s)
    bias_tbl = _band_bias(rel_bias)
    tile2 = lambda v: jnp.tile(v.astype(F32), LANES // HEAD_DIM)
    for l in range(depth):
        shift, scale, gate = (mod[l, :, k * d:(k + 1) * d].reshape(bsz, 1, d) for k in range(3))
        w_main, w_t = _layer_weights(w_in[l])
        head_w = jnp.stack([tile2(q_norm_a[l]), tile2(k_norm_a[l]), tile2(q_norm_b[l]), tile2(k_norm_b[l])]
                           + [jnp.zeros((LANES,), F32)] * 4, axis=0)
        (qa, ka, vt, ga, qb, kb, vb, gb, xbc, zc, dt, mg) = _in_projection(
            x, 1.0 + scale, shift, norm_w[l].reshape(1, d), cos_t, sin_t, head_w, w_main, w_t)
        oa = _mixer_a(qa, ka, vt)
        ob, lse = zip(*[_mixer_b_group(qb, kb, vb, bias_tbl[g], g) for g in range(B_GROUPS)])
        conv_w8 = jnp.pad(conv_w[l], ((0, 8 - C_CONV), (0, 0)))
        xact = _conv_silu(xbc, conv_w8, conv_b[l].reshape(1, C_XBC))
        prm = jnp.stack([_pad_row(dt_bias[l].reshape(-1)), _pad_row(a_log[l].reshape(-1)),
                         _pad_row(jnp.tile(d_skip[l], 2))] + [jnp.zeros((LANES,), F32)] * 5, axis=0)
        yf, yb = _ssd(xact, dt, prm)
        perm_rows = lambda w: w.reshape(A_HEADS, HEAD_DIM, d)[A_HEAD_ORDER, :, :].reshape(A_Q, d)
        x = _merge(x, gate, oa, ga, jnp.stack(ob), jnp.stack(lse), gb, yf, yb, zc, mg,
                   b_gate[l].reshape(1, -1), ssm_norm_w[l].reshape(1, -1),
                   perm_rows(w_proj_a[l]).astype(BF16), w_proj_b[l].astype(BF16),
                   w_proj_c[l].astype(BF16), w_out[l].astype(BF16))
    return x
```

```python
import functools
import math

import numpy as np
import jax
import jax.numpy as jnp
from jax import lax
from jax.experimental import pallas as pl
from jax.experimental.pallas import tpu as pltpu

F32 = jnp.float32
BF16 = jnp.bfloat16

D_MODEL = 1024
HEAD_DIM = 64
EPS = 1e-6
NEG_INF = -1e30
GRID_W = 64
A_HEADS = 8
A_KV_HEADS = 2
ROPE_THETA = 10000.0
B_GROUPS = 3
B_HEADS_PER_GROUP = 4
B_WINDOWS = (128, 512, 2048)
B_DILATIONS = (1, 4, 16)
N_BUCKETS = 32
MAX_DISTANCE = 1024
C_HEADS = 8
C_HEAD_DIM = 64
C_INNER = C_HEADS * C_HEAD_DIM
C_GROUPS = 2
C_STATE = 128
C_CONV = 5
C_CHUNK = 128
N_BRANCH = 3

A_Q = A_HEADS * HEAD_DIM
A_KV = A_KV_HEADS * HEAD_DIM
B_QKV = B_GROUPS * B_HEADS_PER_GROUP * HEAD_DIM
B_OUT = B_HEADS_PER_GROUP * HEAD_DIM
C_BC = C_GROUPS * C_STATE
C_XBC = C_INNER + 2 * C_BC
SPLIT_WIDTHS = (A_Q, A_KV, A_KV, A_Q, B_QKV, B_QKV, B_QKV, B_OUT,
                C_INNER, C_INNER, C_BC, C_BC, C_HEADS, C_HEADS, N_BRANCH * D_MODEL)
SPLIT_OFFSETS = tuple(int(v) for v in np.cumsum((0,) + SPLIT_WIDTHS))

LANES = 128
VMEM_LIMIT = 56 * 1024 * 1024
LOG2E = math.log2(math.e)

_W_GROUPS = (("qa", A_Q), ("ka", A_KV), ("ga", A_Q), ("qb", B_QKV), ("kb", B_QKV), ("vb", B_QKV),
             ("gb", B_OUT), ("xbc", C_XBC), ("zc", C_INNER), ("dt", LANES), ("mg", N_BRANCH * D_MODEL))
_W_OFF = {}
_o = 0
for _n, _w in _W_GROUPS:
    _W_OFF[_n] = (_o, _w)
    _o += _w
W_MAIN_COLS = _o

A_HEAD_ORDER = tuple(h for j in range(4) for h in (j, j + 4))


def _cparams(sem):
    return pltpu.CompilerParams(dimension_semantics=sem, vmem_limit_bytes=VMEM_LIMIT)


def _silu(x):
    return x * jax.nn.sigmoid(x)


def _mod_kernel(c_ref, w_ref, b_ref, o_ref):
    c = c_ref[...]
    o_ref[0] = jnp.dot(_silu(c), w_ref[0], preferred_element_type=F32,
                       precision=lax.Precision.HIGHEST) + b_ref[0]


def _modulation(c_pad, w_ada, b_ada):
    depth, d, n = w_ada.shape
    tn = 512
    return pl.pallas_call(
        _mod_kernel,
        grid=(depth, n // tn),
        in_specs=[pl.BlockSpec((c_pad.shape[0], d), lambda l, j: (0, 0)),
                  pl.BlockSpec((1, d, tn), lambda l, j: (l, 0, j)),
                  pl.BlockSpec((1, 1, tn), lambda l, j: (l, 0, j))],
        out_specs=pl.BlockSpec((1, c_pad.shape[0], tn), lambda l, j: (l, 0, j)),
        out_shape=jax.ShapeDtypeStruct((depth, c_pad.shape[0], n), F32),
        compiler_params=_cparams(("arbitrary", "arbitrary")),
        name="adaln_mod",
    )(c_pad, w_ada, b_ada.reshape(depth, 1, n))


def _head_sumsq_matrix():
    r = lax.broadcasted_iota(jnp.int32, (LANES, LANES), 0) // HEAD_DIM
    c = lax.broadcasted_iota(jnp.int32, (LANES, LANES), 1) // HEAD_DIM
    return jnp.where(r == c, 1.0 / HEAD_DIM, 0.0).astype(BF16)


def _inproj_kernel(x_ref, sc_ref, sh_ref, nw_ref, cos_ref, sin_ref, hw_ref, w_ref, wt_ref,
                   qa_ref, ka_ref, vt_ref, ga_ref,
                   qb0_ref, qb1_ref, qb2_ref, kb0_ref, kb1_ref, kb2_ref, vb0_ref, vb1_ref, vb2_ref,
                   gb_ref, xbc_ref, zc_ref, dt_ref, mg_ref, h_ref):
    x = x_ref[0]
    tm = x.shape[0]
    ms = jnp.mean(x * x, axis=-1, keepdims=True)
    h = (x * lax.rsqrt(ms + EPS)) * nw_ref[...]
    h = h * sc_ref[0] + sh_ref[0]
    hb = h.astype(BF16)
    for j in range(D_MODEL // LANES):
        h_ref[j] = h[:, j * LANES:(j + 1) * LANES]

    bd = _head_sumsq_matrix()
    lane = lax.broadcasted_iota(jnp.int32, (1, LANES), 1)
    first_half = (lane % 32) < 16
    cos = cos_ref[...]
    sin = sin_ref[...]

    def proj(name, c0, c1):
        off, _ = _W_OFF[name]
        return jnp.dot(hb, w_ref[:, off + c0:off + c1], preferred_element_type=F32)

    def head_norm(y, w_row):
        ss = jnp.dot((y * y).astype(BF16), bd, preferred_element_type=F32)
        return (y * lax.rsqrt(ss + EPS)) * w_row

    def rope(y):
        sw = jnp.where(first_half, pltpu.roll(y, LANES - 16, 1), pltpu.roll(y, 16, 1))
        return y * cos + sw * sin

    for j in range(A_Q // LANES):
        y = rope(head_norm(proj("qa", j * LANES, (j + 1) * LANES), hw_ref[0:1, :]))
        qa_ref[0, :, j * LANES:(j + 1) * LANES] = (y * (HEAD_DIM ** -0.5 * LOG2E)).astype(BF16)
    ka_ref[0] = rope(head_norm(proj("ka", 0, LANES), hw_ref[1:2, :])).astype(BF16)
    vt_ref[0] = lax.dot_general(wt_ref[...], hb, (((1,), (1,)), ((), ())),
                                preferred_element_type=F32).astype(BF16)
    for j in range(0, A_Q, 256):
        ga_ref[0, :, j:j + 256] = proj("ga", j, j + 256).astype(BF16)
    def store_phases(ref, lanes, y, d):
        n = tm // d
        for r in range(d):
            ref[0, r, :, lanes] = y[r * n:(r + 1) * n, :]

    for g, (qr, kr, vr) in enumerate(((qb0_ref, kb0_ref, vb0_ref), (qb1_ref, kb1_ref, vb1_ref),
                                      (qb2_ref, kb2_ref, vb2_ref))):
        d = B_DILATIONS[g]
        if d == 1:
            lhs = hb
        else:
            lhs = jnp.concatenate(
                [jnp.concatenate([h_ref[j, pl.ds(r, tm // d, stride=d), :] for r in range(d)], axis=0)
                 for j in range(D_MODEL // LANES)], axis=1).astype(BF16)

        def proj_g(name, c0, c1, lhs=lhs):
            off, _ = _W_OFF[name]
            return jnp.dot(lhs, w_ref[:, off + c0:off + c1], preferred_element_type=F32)

        for j in range(B_OUT // LANES):
            c0 = g * B_OUT + j * LANES
            lanes = slice(j * LANES, (j + 1) * LANES)
            q = head_norm(proj_g("qb", c0, c0 + LANES), hw_ref[2:3, :]) * (HEAD_DIM ** -0.5)
            store_phases(qr, lanes, q.astype(BF16), d)
            k = head_norm(proj_g("kb", c0, c0 + LANES), hw_ref[3:4, :])
            store_phases(kr, lanes, k.astype(BF16), d)
        v = proj_g("vb", g * B_OUT, (g + 1) * B_OUT)
        store_phases(vr, slice(0, B_OUT), v.astype(BF16), d)
    gb_ref[0] = proj("gb", 0, B_OUT).astype(BF16)
    for j in range(0, C_XBC, 256):
        xbc_ref[0, :, j:j + 256] = proj("xbc", j, j + 256).astype(BF16)
    for j in range(0, C_INNER, 256):
        zc_ref[0, :, j:j + 256] = proj("zc", j, j + 256).astype(BF16)
    dt_ref[0] = proj("dt", 0, LANES)
    for j in range(0, N_BRANCH * D_MODEL, 512):
        mg_ref[0, :, j:j + 512] = proj("mg", j, j + 512).astype(BF16)


def _in_projection(x, sc1p, shift, norm_w, cos_t, sin_t, head_w, w_main, w_t, tm=256):
    bsz, s, d = x.shape
    nt = s // tm
    row = lambda i, w: pl.BlockSpec((1, tm, w), lambda b, i: (b, i, 0))
    const2 = lambda shp: pl.BlockSpec(shp, lambda b, i: (0, 0))
    phase = [("phase", dil) for _ in range(3) for dil in B_DILATIONS]
    outs = ([("qa", A_Q, BF16), ("ka", A_KV, BF16), None, ("ga", A_Q, BF16)] + phase +
            [("gb", B_OUT, BF16), ("xbc", C_XBC, BF16),
             ("zc", C_INNER, BF16), ("dt", LANES, F32), ("mg", N_BRANCH * D_MODEL, BF16)])
    out_shape, out_specs = [], []
    for o in outs:
        if o is None:
            out_shape.append(jax.ShapeDtypeStruct((bsz, A_KV, s), BF16))
            out_specs.append(pl.BlockSpec((1, A_KV, tm), lambda b, i: (b, 0, i)))
        elif o[0] == "phase":
            out_shape.append(jax.ShapeDtypeStruct((bsz, o[1], s // o[1], B_OUT), BF16))
            out_specs.append(pl.BlockSpec((1, o[1], tm // o[1], B_OUT), lambda b, i: (b, 0, i, 0)))
        else:
            out_shape.append(jax.ShapeDtypeStruct((bsz, s, o[1]), o[2]))
            out_specs.append(row(0, o[1]))
    return pl.pallas_call(
        _inproj_kernel,
        grid=(bsz, nt),
        in_specs=[row(0, d),
                  pl.BlockSpec((1, 1, d), lambda b, i: (b, 0, 0)),
                  pl.BlockSpec((1, 1, d), lambda b, i: (b, 0, 0)),
                  const2((1, d)),
                  pl.BlockSpec((tm, LANES), lambda b, i: (i, 0)),
                  pl.BlockSpec((tm, LANES), lambda b, i: (i, 0)),
                  const2((8, LANES)),
                  const2(w_main.shape),
                  const2(w_t.shape)],
        out_specs=out_specs,
        out_shape=out_shape,
        scratch_shapes=[pltpu.VMEM((d // LANES, tm, LANES), F32)],
        compiler_params=_cparams(("arbitrary", "arbitrary")),
        name="norm_inproj",
    )(x, sc1p, shift, norm_w, cos_t, sin_t, head_w, w_main, w_t)


A_AUG = HEAD_DIM + 8


def _attn_a_kernel(q_ref, k_ref, vt_ref, o_ref, qt_ref, m_ref, acc_ref, st_ref, *, kc):
    tq = q_ref.shape[1]
    s_len = k_ref.shape[1]
    n_blk = A_Q // LANES
    lane = lax.broadcasted_iota(jnp.int32, (tq, LANES), 1)
    for j in range(n_blk):
        qblk = q_ref[0, :, j * LANES:(j + 1) * LANES]
        for kv in range(A_KV_HEADS):
            in_half = (lane < HEAD_DIM) if kv == 0 else (lane >= HEAD_DIM)
            qt_ref[A_KV_HEADS * j + kv] = jnp.where(in_half, qblk, jnp.zeros_like(qblk))
    m_ref[...] = jnp.full(m_ref.shape, NEG_INF, F32)
    acc_ref[...] = jnp.zeros(acc_ref.shape, F32)
    ones = jnp.ones((A_AUG - HEAD_DIM, kc), BF16)

    n_chunks = s_len // kc

    def scores(c, h):
        off = pl.multiple_of(c * kc, kc)
        return lax.dot_general(k_ref[0, pl.ds(off, kc), :], qt_ref[h], (((1,), (1,)), ((), ())),
                               preferred_element_type=F32)

    ahead = st_ref.shape[0]
    for i in range(ahead):
        st_ref[i] = scores(0, i)

    def body(c, carry):
        off = pl.multiple_of(c * kc, kc)
        vaug = [jnp.concatenate([vt_ref[0, kv * HEAD_DIM:(kv + 1) * HEAD_DIM, pl.ds(off, kc)], ones], axis=0)
                for kv in range(A_KV_HEADS)]
        pending = [st_ref[i] for i in range(ahead)]
        c_next = jnp.minimum(c + 1, n_chunks - 1)
        for h in range(A_HEADS):
            nxt = h + ahead
            pending.append(scores(c, nxt) if nxt < A_HEADS else scores(c_next, nxt - A_HEADS))
            st = pending.pop(0)
            m = m_ref[h:h + 1, :]
            m_new = jnp.maximum(m, jnp.max(st, axis=0, keepdims=True))
            alpha = jnp.exp2(m - m_new)
            p = jnp.exp2(st - m_new).astype(BF16)
            acc_ref[h] = alpha * acc_ref[h] + jnp.dot(vaug[h % A_KV_HEADS], p, preferred_element_type=F32)
            m_ref[h:h + 1, :] = m_new
        for i in range(ahead):
            st_ref[i] = pending[i]
        return carry

    lax.fori_loop(0, n_chunks, body, 0)
    for j in range(n_blk):
        halves = []
        for kv in range(A_KV_HEADS):
            a = acc_ref[A_KV_HEADS * j + kv]
            halves.append(a[:HEAD_DIM] / a[HEAD_DIM:HEAD_DIM + 1])
        o_ref[0, :, j * LANES:(j + 1) * LANES] = jnp.concatenate(halves, axis=0).T.astype(o_ref.dtype)


def _mixer_a(qa, ka, vt, tq=256, kc=512):
    bsz, s, _ = qa.shape
    return pl.pallas_call(
        functools.partial(_attn_a_kernel, kc=kc),
        grid=(bsz, s // tq),
        in_specs=[pl.BlockSpec((1, tq, A_Q), lambda b, i: (b, i, 0)),
                  pl.BlockSpec((1, s, A_KV), lambda b, i: (b, 0, 0)),
                  pl.BlockSpec((1, A_KV, s), lambda b, i: (b, 0, 0))],
        out_specs=pl.BlockSpec((1, tq, A_Q), lambda b, i: (b, i, 0)),
        out_shape=jax.ShapeDtypeStruct((bsz, s, A_Q), BF16),
        scratch_shapes=[pltpu.VMEM((A_HEADS, tq, LANES), BF16),
                        pltpu.VMEM((A_HEADS, tq), F32),
                        pltpu.VMEM((A_HEADS, A_AUG, tq), F32),
                        pltpu.VMEM((3, kc, tq), F32)],
        compiler_params=_cparams(("arbitrary", "arbitrary")),
        name="mixer_a_attn",
    )(qa, ka, vt)


B_SIDE = 64
B_TQ = 256
B_BAND = B_TQ + 2 * B_SIDE


def _attn_b_kernel(q_ref, kp_ref, kc_ref, kn_ref, vp_ref, vc_ref, vn_ref, bias_ref, o_ref, lse_ref):
    mi = pl.program_id(2)
    n_m = pl.num_programs(2)
    q = q_ref[0, 0]
    kband = jnp.concatenate([kp_ref[0, 0], kc_ref[0, 0], kn_ref[0, 0]], axis=0)
    vband = jnp.concatenate([vp_ref[0, 0], vc_ref[0, 0], vn_ref[0, 0]], axis=0)
    kk = lax.broadcasted_iota(jnp.int32, (1, B_BAND), 1)
    key_ok = jnp.logical_and(jnp.logical_or(kk >= B_SIDE, mi > 0),
                             jnp.logical_or(kk < B_SIDE + B_TQ, mi < n_m - 1))
    lane = lax.broadcasted_iota(jnp.int32, (B_TQ, B_OUT), 1) // HEAD_DIM
    o_acc = jnp.zeros((B_TQ, B_OUT), F32)
    lse_acc = jnp.zeros((B_TQ, B_OUT), F32)
    for h in range(B_HEADS_PER_GROUP):
        qh = jnp.where(lane == h, q, jnp.zeros_like(q))
        s = lax.dot_general(qh, kband, (((1,), (1,)), ((), ())), preferred_element_type=F32)
        s = jnp.where(key_ok, s + bias_ref[h], NEG_INF)
        mx = jnp.max(s, axis=-1, keepdims=True)
        e = jnp.exp(s - mx)
        den = jnp.sum(e, axis=-1, keepdims=True)
        p = (e / den).astype(BF16)
        oh = jnp.dot(p, vband, preferred_element_type=F32)
        o_acc = jnp.where(lane == h, oh, o_acc)
        lse_acc = jnp.where(lane == h, mx + jnp.log(den), lse_acc)
    o_ref[0, 0] = o_acc.astype(o_ref.dtype)
    lse_ref[0, 0] = lse_acc


def _mixer_b_group(qb, kb, vb, bias_g, g):
    bsz, d, m, _ = qb.shape
    n_m = m // B_TQ
    hb = B_TQ // B_SIDE
    n_side = m // B_SIDE
    cur = pl.BlockSpec((1, 1, B_TQ, B_OUT), lambda b, r, i: (b, r, i, 0))
    prev = pl.BlockSpec((1, 1, B_SIDE, B_OUT), lambda b, r, i: (b, r, jnp.maximum(i * hb - 1, 0), 0))
    nxt = pl.BlockSpec((1, 1, B_SIDE, B_OUT),
                       lambda b, r, i: (b, r, jnp.minimum((i + 1) * hb, n_side - 1), 0))
    return pl.pallas_call(
        _attn_b_kernel,
        grid=(bsz, d, n_m),
        in_specs=[cur, prev, cur, nxt, prev, cur, nxt,
                  pl.BlockSpec((B_HEADS_PER_GROUP, B_TQ, B_BAND), lambda b, r, i: (0, 0, 0))],
        out_specs=[cur, cur],
        out_shape=[jax.ShapeDtypeStruct((bsz, d, m, B_OUT), F32)] * 2,
        compiler_params=_cparams(("arbitrary", "arbitrary", "arbitrary")),
        name=f"mixer_b_attn_g{g}",
    )(qb, kb, kb, kb, vb, vb, vb, bias_g)


def _t5_bucket(rel):
    nb = N_BUCKETS // 2
    max_exact = nb // 2
    ret = jnp.where(rel > 0, nb, 0)
    n = jnp.abs(rel)
    nf = jnp.maximum(n, 1).astype(F32)
    large = max_exact + (jnp.log(nf / max_exact) / math.log(MAX_DISTANCE / max_exact)
                         * (nb - max_exact)).astype(jnp.int32)
    large = jnp.minimum(large, nb - 1)
    return ret + jnp.where(n < max_exact, n, large)


def _band_bias(rel_bias):
    rel = jnp.arange(-B_SIDE, B_SIDE + 1)
    n_rel = 2 * B_SIDE + 1
    width = B_TQ + B_BAND
    nh = B_HEADS_PER_GROUP
    tables = []
    for g in range(B_GROUPS):
        tbl = rel_bias[:, g * nh:(g + 1) * nh]
        b1 = tbl[_t5_bucket(rel * B_DILATIONS[g])].T.astype(F32)
        w = jnp.concatenate([b1, jnp.full((nh, width - n_rel), NEG_INF, F32)], axis=1)
        flat = jnp.broadcast_to(w[:, None, :], (nh, B_TQ, width)).reshape(nh, B_TQ * width)
        tables.append(flat[:, :B_TQ * (width - 1)].reshape(nh, B_TQ, width - 1)[:, :, :B_BAND])
    return jnp.stack(tables, axis=0)


CONV_HALO = 8


def _conv_kernel(xp_ref, xc_ref, xn_ref, w_ref, b_ref, o_ref, ext_ref):
    i = pl.program_id(1)
    n_t = pl.num_programs(1)
    tc = xc_ref.shape[1]
    prev = jnp.where(i > 0, xp_ref[0].astype(F32), 0.0)
    nxt = jnp.where(i < n_t - 1, xn_ref[0].astype(F32), 0.0)
    ext_ref[0:CONV_HALO, :] = prev
    ext_ref[CONV_HALO:CONV_HALO + tc, :] = xc_ref[0].astype(F32)
    ext_ref[CONV_HALO + tc:, :] = nxt
    acc = jnp.zeros((tc, xc_ref.shape[2]), F32) + b_ref[...]
    for k in range(C_CONV):
        start = CONV_HALO + k - C_CONV // 2
        acc = acc + ext_ref[start:start + tc, :] * w_ref[k:k + 1, :]
    o_ref[0] = _silu(acc).astype(o_ref.dtype)


def _conv_silu(xbc, conv_w, conv_b, tc=512):
    bsz, s, ch = xbc.shape
    n_t = s // tc
    hb = tc // CONV_HALO
    n_h = s // CONV_HALO
    return pl.pallas_call(
        _conv_kernel,
        grid=(bsz, n_t),
        in_specs=[pl.BlockSpec((1, CONV_HALO, ch), lambda b, i: (b, jnp.maximum(i * hb - 1, 0), 0)),
                  pl.BlockSpec((1, tc, ch), lambda b, i: (b, i, 0)),
                  pl.BlockSpec((1, CONV_HALO, ch), lambda b, i: (b, jnp.minimum((i + 1) * hb, n_h - 1), 0)),
                  pl.BlockSpec((8, ch), lambda b, i: (0, 0)),
                  pl.BlockSpec((1, ch), lambda b, i: (0, 0))],
        out_specs=pl.BlockSpec((1, tc, ch), lambda b, i: (b, i, 0)),
        out_shape=jax.ShapeDtypeStruct((bsz, s, ch), BF16),
        scratch_shapes=[pltpu.VMEM((tc + 2 * CONV_HALO, ch), F32)],
        compiler_params=_cparams(("arbitrary", "arbitrary")),
        name="mixer_c_conv",
    )(xbc, xbc, xbc, conv_w, conv_b)


def _split3(x):
    hi = x.astype(BF16)
    r1 = x - hi.astype(F32)
    mid = r1.astype(BF16)
    lo = (r1 - mid.astype(F32)).astype(BF16)
    return hi, mid, lo


def _ssd_direction(x_ref, dt_ref, prm_ref, state_ref, y_ref, *, reverse):
    L = C_CHUNK
    col0 = C_HEADS if reverse else 0
    r = lax.broadcasted_iota(jnp.int32, (L, L), 0)
    c = lax.broadcasted_iota(jnp.int32, (L, L), 1)
    causal = (r <= c) if reverse else (r >= c)
    tri = jnp.where(causal, 1.0, 0.0).astype(BF16)

    dt_bias = prm_ref[0:1, :]
    a_neg = -jnp.exp(prm_ref[1:2, :])
    dtv = jax.nn.softplus(dt_ref[0] + dt_bias)
    a = dtv * a_neg
    hi, mid, lo = _split3(a)
    cs = (jnp.dot(tri, hi, preferred_element_type=F32) + jnp.dot(tri, mid, preferred_element_type=F32)
          + jnp.dot(tri, lo, preferred_element_type=F32))
    cs_t = cs.T
    tot = cs[0:1, :] if reverse else cs[L - 1:L, :]
    decay_in = jnp.exp(cs)
    decay_out = jnp.exp(tot - cs)
    chunk_decay = jnp.exp(tot)

    lane_head = lax.broadcasted_iota(jnp.int32, (L, 4 * C_HEAD_DIM), 1) // C_HEAD_DIM

    def expand(m, g):
        out = jnp.zeros((m.shape[0], 4 * C_HEAD_DIM), F32)
        for j in range(4):
            cj = col0 + g * 4 + j
            out = jnp.where(lane_head[:m.shape[0]] == j, m[:, cj:cj + 1], out)
        return out

    for g in range(C_GROUPS):
        xg = x_ref[0, :, g * 256:(g + 1) * 256].astype(F32)
        bm = x_ref[0, :, C_INNER + g * C_STATE:C_INNER + (g + 1) * C_STATE]
        cm = x_ref[0, :, C_INNER + C_BC + g * C_STATE:C_INNER + C_BC + (g + 1) * C_STATE]
        cb = lax.dot_general(cm, bm, (((1,), (1,)), ((), ())), preferred_element_type=F32)
        xdt = xg * expand(dtv, g)
        state = state_ref[g]
        y = jnp.dot(cm, state.astype(BF16), preferred_element_type=F32) * expand(decay_in, g)
        xdt_b = xdt.astype(BF16)
        parts = []
        for j in range(4):
            cj = col0 + g * 4 + j
            diff = cs[:, cj:cj + 1] - cs_t[cj:cj + 1, :]
            lm = jnp.exp(jnp.where(causal, diff, NEG_INF))
            gmat = (cb * lm).astype(BF16)
            parts.append(jnp.dot(gmat, xdt_b[:, j * C_HEAD_DIM:(j + 1) * C_HEAD_DIM],
                                 preferred_element_type=F32))
        y = y + jnp.concatenate(parts, axis=1)
        if not reverse:
            y = y + xg * expand(prm_ref[2:3, :], g)
        y_ref[0, :, g * 256:(g + 1) * 256] = y
        xs = (xdt * expand(decay_out, g)).astype(BF16)
        upd = lax.dot_general(bm, xs, (((0,), (0,)), ((), ())), preferred_element_type=F32)
        state_ref[g] = state * expand(chunk_decay, g) + upd


def _ssd_kernel(xf_ref, xb_ref, dtf_ref, dtb_ref, prm_ref, yf_ref, yb_ref, sf_ref, sb_ref):
    @pl.when(pl.program_id(1) == 0)
    def _():
        sf_ref[...] = jnp.zeros_like(sf_ref)
        sb_ref[...] = jnp.zeros_like(sb_ref)

    _ssd_direction(xf_ref, dtf_ref, prm_ref, sf_ref, yf_ref, reverse=False)
    _ssd_direction(xb_ref, dtb_ref, prm_ref, sb_ref, yb_ref, reverse=True)


def _ssd(xact, dt, prm):
    bsz, s, ch = xact.shape
    n_c = s // C_CHUNK
    fwd = lambda w: pl.BlockSpec((1, C_CHUNK, w), lambda b, i: (b, i, 0))
    bwd = lambda w: pl.BlockSpec((1, C_CHUNK, w), lambda b, i: (b, n_c - 1 - i, 0))
    return pl.pallas_call(
        _ssd_kernel,
        grid=(bsz, n_c),
        in_specs=[fwd(ch), bwd(ch), fwd(LANES), bwd(LANES), pl.BlockSpec((8, LANES), lambda b, i: (0, 0))],
        out_specs=[fwd(C_INNER), bwd(C_INNER)],
        out_shape=[jax.ShapeDtypeStruct((bsz, s, C_INNER), F32)] * 2,
        scratch_shapes=[pltpu.VMEM((C_GROUPS, C_STATE, 4 * C_HEAD_DIM), F32)] * 2,
        compiler_params=_cparams(("arbitrary", "arbitrary")),
        name="mixer_c_ssd",
    )(xact, xact, dt, dt, prm)


def _merge_kernel(x_ref, gate_ref, oa_ref, ga_ref, ob0_ref, ob1_ref, ob2_ref, ls0_ref, ls1_ref, ls2_ref,
                  gb_ref, yf_ref, yb_ref, zc_ref, mg_ref, bg_ref, nw_ref, wa_ref, wb_ref, wc_ref, wo_ref,
                  o_ref, il_ref):
    ya = (oa_ref[0].astype(F32) * _silu(ga_ref[0].astype(F32))).astype(BF16)

    def token_order(ref, slot):
        d, n = ref.shape[1], ref.shape[2]
        if d == 1:
            return ref[0, 0]
        halves = B_OUT // LANES
        for r in range(d):
            for j in range(halves):
                il_ref[slot * halves + j, pl.ds(r, n, stride=d), :] = ref[0, r, :, j * LANES:(j + 1) * LANES]
        return jnp.concatenate([il_ref[slot * halves + j] for j in range(halves)], axis=1)

    l0, l1, l2 = token_order(ls0_ref, 0), token_order(ls1_ref, 0), token_order(ls2_ref, 1)
    o0, o1, o2 = token_order(ob0_ref, 0), token_order(ob1_ref, 2), token_order(ob2_ref, 3)
    mx = jnp.maximum(jnp.maximum(l0, l1), l2)
    e0, e1, e2 = jnp.exp(l0 - mx), jnp.exp(l1 - mx), jnp.exp(l2 - mx)
    den = e0 + e1 + e2
    yb = (e0 / den) * o0 + (e1 / den) * o1 + (e2 / den) * o2
    yb = (yb * _silu(gb_ref[0].astype(F32))).astype(BF16)
    y = (yf_ref[0] + yb_ref[0]) * _silu(zc_ref[0].astype(F32))
    yc = ((y * lax.rsqrt(jnp.mean(y * y, axis=-1, keepdims=True) + EPS)) * nw_ref[...]).astype(BF16)

    d = D_MODEL
    gates = lambda k: jax.nn.sigmoid(mg_ref[0, :, k * d:(k + 1) * d].astype(F32) + bg_ref[:, k * d:(k + 1) * d])
    merged = gates(0) * jnp.dot(ya, wa_ref[...], preferred_element_type=F32)
    merged = merged + gates(1) * jnp.dot(yb, wb_ref[...], preferred_element_type=F32)
    merged = merged + gates(2) * jnp.dot(yc, wc_ref[...], preferred_element_type=F32)
    out = jnp.dot(merged.astype(BF16), wo_ref[...], preferred_element_type=F32)
    o_ref[0] = x_ref[0] + gate_ref[0] * out


def _merge(x, gate, oa, ga, ob, lse, gb, yf, yb, zc, mg, b_gate, ssm_w, wa, wb, wc, wo, tm=256):
    bsz, s, d = x.shape
    row = lambda w: pl.BlockSpec((1, tm, w), lambda b, i: (b, i, 0))
    phase = [pl.BlockSpec((1, dil, tm // dil, B_OUT), lambda b, i: (b, 0, i, 0)) for dil in B_DILATIONS]
    const2 = lambda shp: pl.BlockSpec(shp, lambda b, i: (0, 0))
    return pl.pallas_call(
        _merge_kernel,
        grid=(bsz, s // tm),
        in_specs=[row(d), pl.BlockSpec((1, 1, d), lambda b, i: (b, 0, 0)),
                  row(A_Q), row(A_Q)] + phase + phase +
                 [row(B_OUT), row(C_INNER), row(C_INNER), row(C_INNER),
                  row(N_BRANCH * d), const2((1, N_BRANCH * d)), const2((1, C_INNER)),
                  const2(wa.shape), const2(wb.shape), const2(wc.shape), const2(wo.shape)],
        out_specs=row(d),
        out_shape=jax.ShapeDtypeStruct((bsz, s, d), F32),
        scratch_shapes=[pltpu.VMEM((4 * B_OUT // LANES, tm, LANES), F32)],
        compiler_params=_cparams(("arbitrary", "arbitrary")),
        name="merge_out",
    )(x, gate, oa, ga, *ob, *lse, gb, yf, yb, zc, mg, b_gate, ssm_w, wa, wb, wc, wo)


def _rope_tables(s):
    quarter = HEAD_DIM // 4
    freqs = ROPE_THETA ** (-jnp.arange(quarter, dtype=F32) / quarter)
    t = jnp.arange(s)
    ang_r = (t // GRID_W).astype(F32)[:, None] * freqs
    ang_c = (t % GRID_W).astype(F32)[:, None] * freqs
    cos = jnp.concatenate([jnp.cos(ang_r)] * 2 + [jnp.cos(ang_c)] * 2, axis=-1)
    sin = jnp.concatenate([-jnp.sin(ang_r), jnp.sin(ang_r), -jnp.sin(ang_c), jnp.sin(ang_c)], axis=-1)
    return jnp.tile(cos, (1, LANES // HEAD_DIM)), jnp.tile(sin, (1, LANES // HEAD_DIM))


def _layer_weights(w_in_l):
    cols = lambda k: w_in_l[:, SPLIT_OFFSETS[k]:SPLIT_OFFSETS[k + 1]]
    d = w_in_l.shape[0]
    perm = lambda w: w.reshape(d, A_HEADS, HEAD_DIM)[:, A_HEAD_ORDER, :].reshape(d, A_Q)
    dt_cols = jnp.concatenate([cols(12), cols(13), jnp.zeros((d, LANES - 2 * C_HEADS), w_in_l.dtype)], axis=1)
    pieces = {"qa": perm(cols(0)), "ka": cols(1), "ga": perm(cols(3)), "qb": cols(4), "kb": cols(5),
              "vb": cols(6), "gb": cols(7), "xbc": jnp.concatenate([cols(8), cols(10), cols(11)], axis=1),
              "zc": cols(9), "dt": dt_cols, "mg": cols(14)}
    w_main = jnp.concatenate([pieces[n] for n, _ in _W_GROUPS], axis=1).astype(BF16)
    w_t = cols(2).T.astype(BF16)
    return w_main, w_t


def _pad_row(v, width=LANES):
    return jnp.pad(v.astype(F32), (0, width - v.shape[0]))


def kernel(x, c, norm_w, w_ada, b_ada, w_in, b_gate, q_norm_a, k_norm_a, q_norm_b, k_norm_b, rel_bias,
           conv_w, conv_b, a_log, dt_bias, d_skip, ssm_norm_w, w_proj_a, w_proj_b, w_proj_c, w_out):
    bsz, s, d = x.shape
    depth = w_in.shape[0]
    c_pad = jnp.pad(c, ((0, 8 - bsz), (0, 0)))
    mod = _modulation(c_pad, w_ada, b_ada)[:, :bsz]
    cos_t, sin_t = _rope_tables(s)
    bias_tbl = _band_bias(rel_bias)
    tile2 = lambda v: jnp.tile(v.astype(F32), LANES // HEAD_DIM)
    for l in range(depth):
        shift, scale, gate = (mod[l, :, k * d:(k + 1) * d].reshape(bsz, 1, d) for k in range(3))
        w_main, w_t = _layer_weights(w_in[l])
        head_w = jnp.stack([tile2(q_norm_a[l]), tile2(k_norm_a[l]), tile2(q_norm_b[l]), tile2(k_norm_b[l])]
                           + [jnp.zeros((LANES,), F32)] * 4, axis=0)
        (qa, ka, vt, ga, *qkv_b, gb, xbc, zc, dt, mg) = _in_projection(
            x, 1.0 + scale, shift, norm_w[l].reshape(1, d), cos_t, sin_t, head_w, w_main, w_t)
        oa = _mixer_a(qa, ka, vt)
        ob, lse = zip(*[_mixer_b_group(qkv_b[g], qkv_b[B_GROUPS + g], qkv_b[2 * B_GROUPS + g], bias_tbl[g], g)
                        for g in range(B_GROUPS)])
        conv_w8 = jnp.pad(conv_w[l], ((0, 8 - C_CONV), (0, 0)))
        xact = _conv_silu(xbc, conv_w8, conv_b[l].reshape(1, C_XBC))
        prm = jnp.stack([_pad_row(dt_bias[l].reshape(-1)), _pad_row(a_log[l].reshape(-1)),
                         _pad_row(jnp.tile(d_skip[l], 2))] + [jnp.zeros((LANES,), F32)] * 5, axis=0)
        yf, yb = _ssd(xact, dt, prm)
        perm_rows = lambda w: w.reshape(A_HEADS, HEAD_DIM, d)[A_HEAD_ORDER, :, :].reshape(A_Q, d)
        x = _merge(x, gate, oa, ga, ob, lse, gb, yf, yb, zc, mg,
                   b_gate[l].reshape(1, -1), ssm_norm_w[l].reshape(1, -1),
                   perm_rows(w_proj_a[l]).astype(BF16), w_proj_b[l].astype(BF16),
                   w_proj_c[l].astype(BF16), w_out[l].astype(BF16))
    return x
```

```python
import functools
import math

import numpy as np
import jax
import jax.numpy as jnp
from jax import lax
from jax.experimental import pallas as pl
from jax.experimental.pallas import tpu as pltpu

F32 = jnp.float32
BF16 = jnp.bfloat16

D_MODEL = 1024
HEAD_DIM = 64
EPS = 1e-6
NEG_INF = -1e30
GRID_W = 64
A_HEADS = 8
A_KV_HEADS = 2
ROPE_THETA = 10000.0
B_GROUPS = 3
B_HEADS_PER_GROUP = 4
B_WINDOWS = (128, 512, 2048)
B_DILATIONS = (1, 4, 16)
N_BUCKETS = 32
MAX_DISTANCE = 1024
C_HEADS = 8
C_HEAD_DIM = 64
C_INNER = C_HEADS * C_HEAD_DIM
C_GROUPS = 2
C_STATE = 128
C_CONV = 5
C_CHUNK = 128
N_BRANCH = 3

A_Q = A_HEADS * HEAD_DIM
A_KV = A_KV_HEADS * HEAD_DIM
B_QKV = B_GROUPS * B_HEADS_PER_GROUP * HEAD_DIM
B_OUT = B_HEADS_PER_GROUP * HEAD_DIM
C_BC = C_GROUPS * C_STATE
C_XBC = C_INNER + 2 * C_BC
SPLIT_WIDTHS = (A_Q, A_KV, A_KV, A_Q, B_QKV, B_QKV, B_QKV, B_OUT,
                C_INNER, C_INNER, C_BC, C_BC, C_HEADS, C_HEADS, N_BRANCH * D_MODEL)
SPLIT_OFFSETS = tuple(int(v) for v in np.cumsum((0,) + SPLIT_WIDTHS))

LANES = 128
VMEM_LIMIT = 56 * 1024 * 1024
LOG2E = math.log2(math.e)

_W_GROUPS = (("qa", A_Q), ("ka", A_KV), ("ga", A_Q), ("qb", B_QKV), ("kb", B_QKV), ("vb", B_QKV),
             ("gb", B_OUT), ("xbc", C_XBC), ("zc", C_INNER), ("dt", LANES), ("mg", N_BRANCH * D_MODEL))
_W_OFF = {}
_o = 0
for _n, _w in _W_GROUPS:
    _W_OFF[_n] = (_o, _w)
    _o += _w
W_MAIN_COLS = _o

A_HEAD_ORDER = tuple(h for j in range(4) for h in (j, j + 4))


def _cparams(sem):
    return pltpu.CompilerParams(dimension_semantics=sem, vmem_limit_bytes=VMEM_LIMIT)


def _silu(x):
    return x * jax.nn.sigmoid(x)


def _mod_kernel(c_ref, w_ref, b_ref, o_ref):
    c = c_ref[...]
    o_ref[0] = jnp.dot(_silu(c), w_ref[0], preferred_element_type=F32,
                       precision=lax.Precision.HIGHEST) + b_ref[0]


def _modulation(c_pad, w_ada, b_ada):
    depth, d, n = w_ada.shape
    tn = 512
    return pl.pallas_call(
        _mod_kernel,
        grid=(depth, n // tn),
        in_specs=[pl.BlockSpec((c_pad.shape[0], d), lambda l, j: (0, 0)),
                  pl.BlockSpec((1, d, tn), lambda l, j: (l, 0, j)),
                  pl.BlockSpec((1, 1, tn), lambda l, j: (l, 0, j))],
        out_specs=pl.BlockSpec((1, c_pad.shape[0], tn), lambda l, j: (l, 0, j)),
        out_shape=jax.ShapeDtypeStruct((depth, c_pad.shape[0], n), F32),
        compiler_params=_cparams(("arbitrary", "arbitrary")),
        name="adaln_mod",
    )(c_pad, w_ada, b_ada.reshape(depth, 1, n))


def _head_sumsq_matrix():
    r = lax.broadcasted_iota(jnp.int32, (LANES, LANES), 0) // HEAD_DIM
    c = lax.broadcasted_iota(jnp.int32, (LANES, LANES), 1) // HEAD_DIM
    return jnp.where(r == c, 1.0 / HEAD_DIM, 0.0).astype(BF16)


def _inproj_kernel(x_ref, sc_ref, sh_ref, nw_ref, cos_ref, sin_ref, hw_ref, w_ref, wt_ref,
                   qa_ref, ka_ref, vt_ref, ga_ref,
                   qb0_ref, qb1_ref, qb2_ref, kb0_ref, kb1_ref, kb2_ref, vb0_ref, vb1_ref, vb2_ref,
                   gb_ref, xbc_ref, zc_ref, dt_ref, mg_ref, h_ref):
    x = x_ref[0]
    tm = x.shape[0]
    ms = jnp.mean(x * x, axis=-1, keepdims=True)
    h = (x * lax.rsqrt(ms + EPS)) * nw_ref[...]
    h = h * sc_ref[0] + sh_ref[0]
    hb = h.astype(BF16)
    for j in range(D_MODEL // LANES):
        h_ref[j] = h[:, j * LANES:(j + 1) * LANES]

    bd = _head_sumsq_matrix()
    lane = lax.broadcasted_iota(jnp.int32, (1, LANES), 1)
    first_half = (lane % 32) < 16
    cos = cos_ref[...]
    sin = sin_ref[...]

    def proj(name, c0, c1):
        off, _ = _W_OFF[name]
        return jnp.dot(hb, w_ref[:, off + c0:off + c1], preferred_element_type=F32)

    def head_norm(y, w_row):
        ss = jnp.dot((y * y).astype(BF16), bd, preferred_element_type=F32)
        return (y * lax.rsqrt(ss + EPS)) * w_row

    def rope(y):
        sw = jnp.where(first_half, pltpu.roll(y, LANES - 16, 1), pltpu.roll(y, 16, 1))
        return y * cos + sw * sin

    deferred = []

    def normed(y, finish):
        waiting = list(deferred)
        deferred[:] = [(y, finish)]
        for py, pfinish in waiting:
            pfinish(py)

    def finish_qa(y, j):
        y = rope(head_norm(y, hw_ref[0:1, :]))
        qa_ref[0, :, j * LANES:(j + 1) * LANES] = (y * (HEAD_DIM ** -0.5 * LOG2E)).astype(BF16)

    def finish_ka(y):
        ka_ref[0] = rope(head_norm(y, hw_ref[1:2, :])).astype(BF16)

    for j in range(A_Q // LANES):
        normed(proj("qa", j * LANES, (j + 1) * LANES), functools.partial(finish_qa, j=j))
    normed(proj("ka", 0, LANES), finish_ka)
    vt_ref[0] = lax.dot_general(wt_ref[...], hb, (((1,), (1,)), ((), ())),
                                preferred_element_type=F32).astype(BF16)
    for j in range(0, A_Q, 256):
        ga_ref[0, :, j:j + 256] = proj("ga", j, j + 256).astype(BF16)
    def store_phases(ref, lanes, y, d):
        n = tm // d
        for r in range(d):
            ref[0, r, :, lanes] = y[r * n:(r + 1) * n, :]

    for g, (qr, kr, vr) in enumerate(((qb0_ref, kb0_ref, vb0_ref), (qb1_ref, kb1_ref, vb1_ref),
                                      (qb2_ref, kb2_ref, vb2_ref))):
        d = B_DILATIONS[g]
        if d == 1:
            lhs = hb
        else:
            lhs = jnp.concatenate(
                [jnp.concatenate([h_ref[j, pl.ds(r, tm // d, stride=d), :] for r in range(d)], axis=0)
                 for j in range(D_MODEL // LANES)], axis=1).astype(BF16)

        def proj_g(name, c0, c1, lhs=lhs):
            off, _ = _W_OFF[name]
            return jnp.dot(lhs, w_ref[:, off + c0:off + c1], preferred_element_type=F32)

        def finish_b(y, ref, lanes, w_row, scale, d=d):
            store_phases(ref, lanes, (head_norm(y, w_row) * scale).astype(BF16), d)

        for j in range(B_OUT // LANES):
            c0 = g * B_OUT + j * LANES
            lanes = slice(j * LANES, (j + 1) * LANES)
            normed(proj_g("qb", c0, c0 + LANES), functools.partial(
                finish_b, ref=qr, lanes=lanes, w_row=hw_ref[2:3, :], scale=HEAD_DIM ** -0.5))
            normed(proj_g("kb", c0, c0 + LANES), functools.partial(
                finish_b, ref=kr, lanes=lanes, w_row=hw_ref[3:4, :], scale=1.0))
        v = proj_g("vb", g * B_OUT, (g + 1) * B_OUT)
        store_phases(vr, slice(0, B_OUT), v.astype(BF16), d)
    normed(None, None)
    deferred.clear()
    gb_ref[0] = proj("gb", 0, B_OUT).astype(BF16)
    for j in range(0, C_XBC, 256):
        xbc_ref[0, :, j:j + 256] = proj("xbc", j, j + 256).astype(BF16)
    for j in range(0, C_INNER, 256):
        zc_ref[0, :, j:j + 256] = proj("zc", j, j + 256).astype(BF16)
    dt_ref[0] = proj("dt", 0, LANES)
    for j in range(0, N_BRANCH * D_MODEL, 512):
        mg_ref[0, :, j:j + 512] = proj("mg", j, j + 512).astype(BF16)


def _in_projection(x, sc1p, shift, norm_w, cos_t, sin_t, head_w, w_main, w_t, tm=256):
    bsz, s, d = x.shape
    nt = s // tm
    row = lambda i, w: pl.BlockSpec((1, tm, w), lambda b, i: (b, i, 0))
    const2 = lambda shp: pl.BlockSpec(shp, lambda b, i: (0, 0))
    phase = [("phase", dil) for _ in range(3) for dil in B_DILATIONS]
    outs = ([("qa", A_Q, BF16), ("ka", A_KV, BF16), None, ("ga", A_Q, BF16)] + phase +
            [("gb", B_OUT, BF16), ("xbc", C_XBC, BF16),
             ("zc", C_INNER, BF16), ("dt", LANES, F32), ("mg", N_BRANCH * D_MODEL, BF16)])
    out_shape, out_specs = [], []
    for o in outs:
        if o is None:
            out_shape.append(jax.ShapeDtypeStruct((bsz, A_KV, s), BF16))
            out_specs.append(pl.BlockSpec((1, A_KV, tm), lambda b, i: (b, 0, i)))
        elif o[0] == "phase":
            out_shape.append(jax.ShapeDtypeStruct((bsz, o[1], s // o[1], B_OUT), BF16))
            out_specs.append(pl.BlockSpec((1, o[1], tm // o[1], B_OUT), lambda b, i: (b, 0, i, 0)))
        else:
            out_shape.append(jax.ShapeDtypeStruct((bsz, s, o[1]), o[2]))
            out_specs.append(row(0, o[1]))
    return pl.pallas_call(
        _inproj_kernel,
        grid=(bsz, nt),
        in_specs=[row(0, d),
                  pl.BlockSpec((1, 1, d), lambda b, i: (b, 0, 0)),
                  pl.BlockSpec((1, 1, d), lambda b, i: (b, 0, 0)),
                  const2((1, d)),
                  pl.BlockSpec((tm, LANES), lambda b, i: (i, 0)),
                  pl.BlockSpec((tm, LANES), lambda b, i: (i, 0)),
                  const2((8, LANES)),
                  const2(w_main.shape),
                  const2(w_t.shape)],
        out_specs=out_specs,
        out_shape=out_shape,
        scratch_shapes=[pltpu.VMEM((d // LANES, tm, LANES), F32)],
        compiler_params=_cparams(("arbitrary", "arbitrary")),
        name="norm_inproj",
    )(x, sc1p, shift, norm_w, cos_t, sin_t, head_w, w_main, w_t)


A_AUG = HEAD_DIM + 8


def _attn_a_kernel(bounded_ref, q_ref, k_ref, vt_ref, o_ref, qt_ref, m_ref, acc_ref, st_ref, *, kc):
    tq = q_ref.shape[1]
    s_len = k_ref.shape[1]
    n_blk = A_Q // LANES
    lane = lax.broadcasted_iota(jnp.int32, (tq, LANES), 1)
    for j in range(n_blk):
        qblk = q_ref[0, :, j * LANES:(j + 1) * LANES]
        for kv in range(A_KV_HEADS):
            in_half = (lane < HEAD_DIM) if kv == 0 else (lane >= HEAD_DIM)
            qt_ref[A_KV_HEADS * j + kv] = jnp.where(in_half, qblk, jnp.zeros_like(qblk))
    m_ref[...] = jnp.full(m_ref.shape, NEG_INF, F32)
    acc_ref[...] = jnp.zeros(acc_ref.shape, F32)
    ones = jnp.ones((A_AUG - HEAD_DIM, kc), BF16)

    n_chunks = s_len // kc

    def scores(c, h):
        off = pl.multiple_of(c * kc, kc)
        return lax.dot_general(k_ref[0, pl.ds(off, kc), :], qt_ref[h], (((1,), (1,)), ((), ())),
                               preferred_element_type=F32)

    ahead = st_ref.shape[0]

    def sweep(online, unroll):
        for i in range(ahead):
            st_ref[i] = scores(0, i)

        def vaug(c, kv):
            off = pl.multiple_of(c * kc, kc)
            return jnp.concatenate([vt_ref[0, kv * HEAD_DIM:(kv + 1) * HEAD_DIM, pl.ds(off, kc)], ones], axis=0)

        def body(cc, carry):
            pending = [st_ref[i] for i in range(ahead)]
            for t in range(unroll * A_HEADS):
                c, h = cc * unroll + t // A_HEADS, t % A_HEADS
                nxt = t + ahead
                c_nxt = jnp.minimum(cc * unroll + nxt // A_HEADS, n_chunks - 1)
                pending.append(scores(c_nxt, nxt % A_HEADS))
                st = pending.pop(0)
                v = vaug(c, h % A_KV_HEADS)
                if online:
                    m = m_ref[h:h + 1, :]
                    m_new = jnp.maximum(m, jnp.max(st, axis=0, keepdims=True))
                    alpha = jnp.exp2(m - m_new)
                    p = jnp.exp2(st - m_new).astype(BF16)
                    acc_ref[h] = alpha * acc_ref[h] + jnp.dot(v, p, preferred_element_type=F32)
                    m_ref[h:h + 1, :] = m_new
                else:
                    p = jnp.exp2(st).astype(BF16)
                    acc_ref[h] += jnp.dot(v, p, preferred_element_type=F32)
            for i in range(ahead):
                st_ref[i] = pending[i]
            return carry

        lax.fori_loop(0, n_chunks // unroll, body, 0)

    bounded = bounded_ref[0] == 1
    pl.when(bounded)(lambda: sweep(False, 4))
    pl.when(jnp.logical_not(bounded))(lambda: sweep(True, 1))

    for j in range(n_blk):
        halves = []
        for kv in range(A_KV_HEADS):
            a = acc_ref[A_KV_HEADS * j + kv]
            halves.append(a[:HEAD_DIM] / a[HEAD_DIM:HEAD_DIM + 1])
        o_ref[0, :, j * LANES:(j + 1) * LANES] = jnp.concatenate(halves, axis=0).T.astype(o_ref.dtype)


A_SCORE_LIMIT = 40.0


def _score_bound(q_norm, k_norm):
    return (1.05 * HEAD_DIM * HEAD_DIM ** -0.5 * LOG2E) * jnp.max(jnp.abs(q_norm)) * jnp.max(jnp.abs(k_norm))


def _mixer_a(qa, ka, vt, bound, tq=256, kc=512):
    bsz, s, _ = qa.shape
    bounded = (bound <= A_SCORE_LIMIT).astype(jnp.int32).reshape(1)
    return pl.pallas_call(
        functools.partial(_attn_a_kernel, kc=kc),
        grid=(bsz, s // tq),
        in_specs=[pl.BlockSpec(memory_space=pltpu.SMEM),
                  pl.BlockSpec((1, tq, A_Q), lambda b, i: (b, i, 0)),
                  pl.BlockSpec((1, s, A_KV), lambda b, i: (b, 0, 0)),
                  pl.BlockSpec((1, A_KV, s), lambda b, i: (b, 0, 0))],
        out_specs=pl.BlockSpec((1, tq, A_Q), lambda b, i: (b, i, 0)),
        out_shape=jax.ShapeDtypeStruct((bsz, s, A_Q), BF16),
        scratch_shapes=[pltpu.VMEM((A_HEADS, tq, LANES), BF16),
                        pltpu.VMEM((A_HEADS, tq), F32),
                        pltpu.VMEM((A_HEADS, A_AUG, tq), F32),
                        pltpu.VMEM((3, kc, tq), F32)],
        compiler_params=_cparams(("arbitrary", "arbitrary")),
        name="mixer_a_attn",
    )(bounded, qa, ka, vt)


B_SIDE = 64
B_TQ = 256
B_BAND = B_TQ + 2 * B_SIDE


def _attn_b_kernel(q_ref, kp_ref, kc_ref, kn_ref, vp_ref, vc_ref, vn_ref, bias_ref, o_ref, lse_ref):
    mi = pl.program_id(2)
    n_m = pl.num_programs(2)
    q = q_ref[0, 0]
    kband = jnp.concatenate([kp_ref[0, 0], kc_ref[0, 0], kn_ref[0, 0]], axis=0)
    vband = jnp.concatenate([vp_ref[0, 0], vc_ref[0, 0], vn_ref[0, 0]], axis=0)
    kk = lax.broadcasted_iota(jnp.int32, (1, B_BAND), 1)
    key_ok = jnp.logical_and(jnp.logical_or(kk >= B_SIDE, mi > 0),
                             jnp.logical_or(kk < B_SIDE + B_TQ, mi < n_m - 1))
    lane = lax.broadcasted_iota(jnp.int32, (B_TQ, B_OUT), 1) // HEAD_DIM
    o_acc = jnp.zeros((B_TQ, B_OUT), F32)
    lse_acc = jnp.zeros((B_TQ, B_OUT), F32)
    for h in range(B_HEADS_PER_GROUP):
        qh = jnp.where(lane == h, q, jnp.zeros_like(q))
        s = lax.dot_general(qh, kband, (((1,), (1,)), ((), ())), preferred_element_type=F32)
        s = jnp.where(key_ok, s + bias_ref[h], NEG_INF)
        mx = jnp.max(s, axis=-1, keepdims=True)
        e = jnp.exp(s - mx)
        den = jnp.sum(e, axis=-1, keepdims=True)
        p = (e / den).astype(BF16)
        oh = jnp.dot(p, vband, preferred_element_type=F32)
        o_acc = jnp.where(lane == h, oh, o_acc)
        lse_acc = jnp.where(lane == h, mx + jnp.log(den), lse_acc)
    o_ref[0, 0] = o_acc.astype(o_ref.dtype)
    lse_ref[0, 0] = lse_acc


def _mixer_b_group(qb, kb, vb, bias_g, g):
    bsz, d, m, _ = qb.shape
    n_m = m // B_TQ
    hb = B_TQ // B_SIDE
    n_side = m // B_SIDE
    cur = pl.BlockSpec((1, 1, B_TQ, B_OUT), lambda b, r, i: (b, r, i, 0))
    prev = pl.BlockSpec((1, 1, B_SIDE, B_OUT), lambda b, r, i: (b, r, jnp.maximum(i * hb - 1, 0), 0))
    nxt = pl.BlockSpec((1, 1, B_SIDE, B_OUT),
                       lambda b, r, i: (b, r, jnp.minimum((i + 1) * hb, n_side - 1), 0))
    return pl.pallas_call(
        _attn_b_kernel,
        grid=(bsz, d, n_m),
        in_specs=[cur, prev, cur, nxt, prev, cur, nxt,
                  pl.BlockSpec((B_HEADS_PER_GROUP, B_TQ, B_BAND), lambda b, r, i: (0, 0, 0))],
        out_specs=[cur, cur],
        out_shape=[jax.ShapeDtypeStruct((bsz, d, m, B_OUT), F32)] * 2,
        compiler_params=_cparams(("arbitrary", "arbitrary", "arbitrary")),
        name=f"mixer_b_attn_g{g}",
    )(qb, kb, kb, kb, vb, vb, vb, bias_g)


def _t5_bucket(rel):
    nb = N_BUCKETS // 2
    max_exact = nb // 2
    ret = jnp.where(rel > 0, nb, 0)
    n = jnp.abs(rel)
    nf = jnp.maximum(n, 1).astype(F32)
    large = max_exact + (jnp.log(nf / max_exact) / math.log(MAX_DISTANCE / max_exact)
                         * (nb - max_exact)).astype(jnp.int32)
    large = jnp.minimum(large, nb - 1)
    return ret + jnp.where(n < max_exact, n, large)


def _band_bias(rel_bias):
    rel = jnp.arange(-B_SIDE, B_SIDE + 1)
    n_rel = 2 * B_SIDE + 1
    width = B_TQ + B_BAND
    nh = B_HEADS_PER_GROUP
    tables = []
    for g in range(B_GROUPS):
        tbl = rel_bias[:, g * nh:(g + 1) * nh]
        b1 = tbl[_t5_bucket(rel * B_DILATIONS[g])].T.astype(F32)
        w = jnp.concatenate([b1, jnp.full((nh, width - n_rel), NEG_INF, F32)], axis=1)
        flat = jnp.broadcast_to(w[:, None, :], (nh, B_TQ, width)).reshape(nh, B_TQ * width)
        tables.append(flat[:, :B_TQ * (width - 1)].reshape(nh, B_TQ, width - 1)[:, :, :B_BAND])
    return jnp.stack(tables, axis=0)


CONV_HALO = 8


def _conv_kernel(xp_ref, xc_ref, xn_ref, w_ref, b_ref, o_ref, ext_ref):
    i = pl.program_id(1)
    n_t = pl.num_programs(1)
    tc = xc_ref.shape[1]
    prev = jnp.where(i > 0, xp_ref[0].astype(F32), 0.0)
    nxt = jnp.where(i < n_t - 1, xn_ref[0].astype(F32), 0.0)
    ext_ref[0:CONV_HALO, :] = prev
    ext_ref[CONV_HALO:CONV_HALO + tc, :] = xc_ref[0].astype(F32)
    ext_ref[CONV_HALO + tc:, :] = nxt
    acc = jnp.zeros((tc, xc_ref.shape[2]), F32) + b_ref[...]
    for k in range(C_CONV):
        start = CONV_HALO + k - C_CONV // 2
        acc = acc + ext_ref[start:start + tc, :] * w_ref[k:k + 1, :]
    o_ref[0] = _silu(acc).astype(o_ref.dtype)


def _conv_silu(xbc, conv_w, conv_b, tc=512):
    bsz, s, ch = xbc.shape
    n_t = s // tc
    hb = tc // CONV_HALO
    n_h = s // CONV_HALO
    return pl.pallas_call(
        _conv_kernel,
        grid=(bsz, n_t),
        in_specs=[pl.BlockSpec((1, CONV_HALO, ch), lambda b, i: (b, jnp.maximum(i * hb - 1, 0), 0)),
                  pl.BlockSpec((1, tc, ch), lambda b, i: (b, i, 0)),
                  pl.BlockSpec((1, CONV_HALO, ch), lambda b, i: (b, jnp.minimum((i + 1) * hb, n_h - 1), 0)),
                  pl.BlockSpec((8, ch), lambda b, i: (0, 0)),
                  pl.BlockSpec((1, ch), lambda b, i: (0, 0))],
        out_specs=pl.BlockSpec((1, tc, ch), lambda b, i: (b, i, 0)),
        out_shape=jax.ShapeDtypeStruct((bsz, s, ch), BF16),
        scratch_shapes=[pltpu.VMEM((tc + 2 * CONV_HALO, ch), F32)],
        compiler_params=_cparams(("arbitrary", "arbitrary")),
        name="mixer_c_conv",
    )(xbc, xbc, xbc, conv_w, conv_b)


def _split3(x):
    hi = x.astype(BF16)
    r1 = x - hi.astype(F32)
    mid = r1.astype(BF16)
    lo = (r1 - mid.astype(F32)).astype(BF16)
    return hi, mid, lo


def _ssd_direction(x_ref, dt_ref, prm_ref, state_ref, y_ref, *, reverse):
    L = C_CHUNK
    col0 = C_HEADS if reverse else 0
    r = lax.broadcasted_iota(jnp.int32, (L, L), 0)
    c = lax.broadcasted_iota(jnp.int32, (L, L), 1)
    causal = (r <= c) if reverse else (r >= c)
    tri = jnp.where(causal, 1.0, 0.0).astype(BF16)

    dt_bias = prm_ref[0:1, :]
    a_neg = -jnp.exp(prm_ref[1:2, :])
    dtv = jax.nn.softplus(dt_ref[0] + dt_bias)
    a = dtv * a_neg
    hi, mid, lo = _split3(a)
    cs = (jnp.dot(tri, hi, preferred_element_type=F32) + jnp.dot(tri, mid, preferred_element_type=F32)
          + jnp.dot(tri, lo, preferred_element_type=F32))
    cs_t = cs.T
    tot = cs[0:1, :] if reverse else cs[L - 1:L, :]
    decay_in = jnp.exp(cs)
    decay_out = jnp.exp(tot - cs)
    chunk_decay = jnp.exp(tot)

    lane_head = lax.broadcasted_iota(jnp.int32, (1, 4 * C_HEAD_DIM), 1) // C_HEAD_DIM

    def expand_row(m, g):
        out = jnp.zeros((1, 4 * C_HEAD_DIM), F32)
        for j in range(4):
            cj = col0 + g * 4 + j
            out = jnp.where(lane_head == j, m[:, cj:cj + 1], out)
        return out

    sel = (lax.broadcasted_iota(jnp.int32, (LANES, C_INNER), 0)
           == col0 + lax.broadcasted_iota(jnp.int32, (LANES, C_INNER), 1) // C_HEAD_DIM)
    sel = jnp.where(sel, 1.0, 0.0).astype(BF16)

    def expand(m):
        hi = m.astype(BF16)
        lo = (m - hi.astype(F32)).astype(BF16)
        return jnp.dot(hi, sel, preferred_element_type=F32) + jnp.dot(lo, sel, preferred_element_type=F32)

    dt_e, din_e, dout_e = expand(dtv), expand(decay_in), expand(decay_out)

    for g in range(C_GROUPS):
        lanes = slice(g * 256, (g + 1) * 256)
        xg = x_ref[0, :, lanes].astype(F32)
        bm = x_ref[0, :, C_INNER + g * C_STATE:C_INNER + (g + 1) * C_STATE]
        cm = x_ref[0, :, C_INNER + C_BC + g * C_STATE:C_INNER + C_BC + (g + 1) * C_STATE]
        cb = lax.dot_general(cm, bm, (((1,), (1,)), ((), ())), preferred_element_type=F32)
        xdt = xg * dt_e[:, lanes]
        state = state_ref[g]
        y = jnp.dot(cm, state.astype(BF16), preferred_element_type=F32) * din_e[:, lanes]
        xdt_b = xdt.astype(BF16)
        parts = []
        for j in range(4):
            cj = col0 + g * 4 + j
            diff = cs[:, cj:cj + 1] - cs_t[cj:cj + 1, :]
            lm = jnp.exp(jnp.where(causal, diff, NEG_INF))
            gmat = (cb * lm).astype(BF16)
            parts.append(jnp.dot(gmat, xdt_b[:, j * C_HEAD_DIM:(j + 1) * C_HEAD_DIM],
                                 preferred_element_type=F32))
        y = y + jnp.concatenate(parts, axis=1)
        if not reverse:
            y = y + xg * expand_row(prm_ref[2:3, :], g)
        y_ref[0, :, lanes] = y
        xs = (xdt * dout_e[:, lanes]).astype(BF16)
        upd = lax.dot_general(bm, xs, (((0,), (0,)), ((), ())), preferred_element_type=F32)
        state_ref[g] = state * expand_row(chunk_decay, g) + upd


def _ssd_kernel(xf_ref, xb_ref, dtf_ref, dtb_ref, prm_ref, yf_ref, yb_ref, sf_ref, sb_ref):
    @pl.when(pl.program_id(1) == 0)
    def _():
        sf_ref[...] = jnp.zeros_like(sf_ref)
        sb_ref[...] = jnp.zeros_like(sb_ref)

    _ssd_direction(xf_ref, dtf_ref, prm_ref, sf_ref, yf_ref, reverse=False)
    _ssd_direction(xb_ref, dtb_ref, prm_ref, sb_ref, yb_ref, reverse=True)


def _ssd(xact, dt, prm):
    bsz, s, ch = xact.shape
    n_c = s // C_CHUNK
    fwd = lambda w: pl.BlockSpec((1, C_CHUNK, w), lambda b, i: (b, i, 0))
    bwd = lambda w: pl.BlockSpec((1, C_CHUNK, w), lambda b, i: (b, n_c - 1 - i, 0))
    return pl.pallas_call(
        _ssd_kernel,
        grid=(bsz, n_c),
        in_specs=[fwd(ch), bwd(ch), fwd(LANES), bwd(LANES), pl.BlockSpec((8, LANES), lambda b, i: (0, 0))],
        out_specs=[fwd(C_INNER), bwd(C_INNER)],
        out_shape=[jax.ShapeDtypeStruct((bsz, s, C_INNER), F32)] * 2,
        scratch_shapes=[pltpu.VMEM((C_GROUPS, C_STATE, 4 * C_HEAD_DIM), F32)] * 2,
        compiler_params=_cparams(("arbitrary", "arbitrary")),
        name="mixer_c_ssd",
    )(xact, xact, dt, dt, prm)


def _merge_kernel(x_ref, gate_ref, oa_ref, ga_ref, ob0_ref, ob1_ref, ob2_ref, ls0_ref, ls1_ref, ls2_ref,
                  gb_ref, yf_ref, yb_ref, zc_ref, mg_ref, bg_ref, nw_ref, wa_ref, wb_ref, wc_ref, wo_ref,
                  o_ref, il_ref):
    ya = (oa_ref[0].astype(F32) * _silu(ga_ref[0].astype(F32))).astype(BF16)

    def token_order(ref, slot):
        d, n = ref.shape[1], ref.shape[2]
        if d == 1:
            return ref[0, 0]
        halves = B_OUT // LANES
        for r in range(d):
            for j in range(halves):
                il_ref[slot * halves + j, pl.ds(r, n, stride=d), :] = ref[0, r, :, j * LANES:(j + 1) * LANES]
        return jnp.concatenate([il_ref[slot * halves + j] for j in range(halves)], axis=1)

    l0, l1, l2 = token_order(ls0_ref, 0), token_order(ls1_ref, 0), token_order(ls2_ref, 1)
    o0, o1, o2 = token_order(ob0_ref, 0), token_order(ob1_ref, 2), token_order(ob2_ref, 3)
    mx = jnp.maximum(jnp.maximum(l0, l1), l2)
    e0, e1, e2 = jnp.exp(l0 - mx), jnp.exp(l1 - mx), jnp.exp(l2 - mx)
    den = e0 + e1 + e2
    yb = (e0 / den) * o0 + (e1 / den) * o1 + (e2 / den) * o2
    yb = (yb * _silu(gb_ref[0].astype(F32))).astype(BF16)
    y = (yf_ref[0] + yb_ref[0]) * _silu(zc_ref[0].astype(F32))
    yc = ((y * lax.rsqrt(jnp.mean(y * y, axis=-1, keepdims=True) + EPS)) * nw_ref[...]).astype(BF16)

    d = D_MODEL
    gates = lambda k: jax.nn.sigmoid(mg_ref[0, :, k * d:(k + 1) * d].astype(F32) + bg_ref[:, k * d:(k + 1) * d])
    merged = gates(0) * jnp.dot(ya, wa_ref[...], preferred_element_type=F32)
    merged = merged + gates(1) * jnp.dot(yb, wb_ref[...], preferred_element_type=F32)
    merged = merged + gates(2) * jnp.dot(yc, wc_ref[...], preferred_element_type=F32)
    out = jnp.dot(merged.astype(BF16), wo_ref[...], preferred_element_type=F32)
    o_ref[0] = x_ref[0] + gate_ref[0] * out


def _merge(x, gate, oa, ga, ob, lse, gb, yf, yb, zc, mg, b_gate, ssm_w, wa, wb, wc, wo, tm=256):
    bsz, s, d = x.shape
    row = lambda w: pl.BlockSpec((1, tm, w), lambda b, i: (b, i, 0))
    phase = [pl.BlockSpec((1, dil, tm // dil, B_OUT), lambda b, i: (b, 0, i, 0)) for dil in B_DILATIONS]
    const2 = lambda shp: pl.BlockSpec(shp, lambda b, i: (0, 0))
    return pl.pallas_call(
        _merge_kernel,
        grid=(bsz, s // tm),
        in_specs=[row(d), pl.BlockSpec((1, 1, d), lambda b, i: (b, 0, 0)),
                  row(A_Q), row(A_Q)] + phase + phase +
                 [row(B_OUT), row(C_INNER), row(C_INNER), row(C_INNER),
                  row(N_BRANCH * d), const2((1, N_BRANCH * d)), const2((1, C_INNER)),
                  const2(wa.shape), const2(wb.shape), const2(wc.shape), const2(wo.shape)],
        out_specs=row(d),
        out_shape=jax.ShapeDtypeStruct((bsz, s, d), F32),
        scratch_shapes=[pltpu.VMEM((4 * B_OUT // LANES, tm, LANES), F32)],
        compiler_params=_cparams(("arbitrary", "arbitrary")),
        name="merge_out",
    )(x, gate, oa, ga, *ob, *lse, gb, yf, yb, zc, mg, b_gate, ssm_w, wa, wb, wc, wo)


def _rope_tables(s):
    quarter = HEAD_DIM // 4
    freqs = ROPE_THETA ** (-jnp.arange(quarter, dtype=F32) / quarter)
    t = jnp.arange(s)
    ang_r = (t // GRID_W).astype(F32)[:, None] * freqs
    ang_c = (t % GRID_W).astype(F32)[:, None] * freqs
    cos = jnp.concatenate([jnp.cos(ang_r)] * 2 + [jnp.cos(ang_c)] * 2, axis=-1)
    sin = jnp.concatenate([-jnp.sin(ang_r), jnp.sin(ang_r), -jnp.sin(ang_c), jnp.sin(ang_c)], axis=-1)
    return jnp.tile(cos, (1, LANES // HEAD_DIM)), jnp.tile(sin, (1, LANES // HEAD_DIM))


def _layer_weights(w_in_l):
    cols = lambda k: w_in_l[:, SPLIT_OFFSETS[k]:SPLIT_OFFSETS[k + 1]]
    d = w_in_l.shape[0]
    perm = lambda w: w.reshape(d, A_HEADS, HEAD_DIM)[:, A_HEAD_ORDER, :].reshape(d, A_Q)
    dt_cols = jnp.concatenate([cols(12), cols(13), jnp.zeros((d, LANES - 2 * C_HEADS), w_in_l.dtype)], axis=1)
    pieces = {"qa": perm(cols(0)), "ka": cols(1), "ga": perm(cols(3)), "qb": cols(4), "kb": cols(5),
              "vb": cols(6), "gb": cols(7), "xbc": jnp.concatenate([cols(8), cols(10), cols(11)], axis=1),
              "zc": cols(9), "dt": dt_cols, "mg": cols(14)}
    w_main = jnp.concatenate([pieces[n] for n, _ in _W_GROUPS], axis=1).astype(BF16)
    w_t = cols(2).T.astype(BF16)
    return w_main, w_t


def _pad_row(v, width=LANES):
    return jnp.pad(v.astype(F32), (0, width - v.shape[0]))


def kernel(x, c, norm_w, w_ada, b_ada, w_in, b_gate, q_norm_a, k_norm_a, q_norm_b, k_norm_b, rel_bias,
           conv_w, conv_b, a_log, dt_bias, d_skip, ssm_norm_w, w_proj_a, w_proj_b, w_proj_c, w_out):
    bsz, s, d = x.shape
    depth = w_in.shape[0]
    c_pad = jnp.pad(c, ((0, 8 - bsz), (0, 0)))
    mod = _modulation(c_pad, w_ada, b_ada)[:, :bsz]
    cos_t, sin_t = _rope_tables(s)
    bias_tbl = _band_bias(rel_bias)
    tile2 = lambda v: jnp.tile(v.astype(F32), LANES // HEAD_DIM)
    for l in range(depth):
        shift, scale, gate = (mod[l, :, k * d:(k + 1) * d].reshape(bsz, 1, d) for k in range(3))
        w_main, w_t = _layer_weights(w_in[l])
        head_w = jnp.stack([tile2(q_norm_a[l]), tile2(k_norm_a[l]), tile2(q_norm_b[l]), tile2(k_norm_b[l])]
                           + [jnp.zeros((LANES,), F32)] * 4, axis=0)
        (qa, ka, vt, ga, *qkv_b, gb, xbc, zc, dt, mg) = _in_projection(
            x, 1.0 + scale, shift, norm_w[l].reshape(1, d), cos_t, sin_t, head_w, w_main, w_t)
        oa = _mixer_a(qa, ka, vt, _score_bound(q_norm_a[l], k_norm_a[l]))
        ob, lse = zip(*[_mixer_b_group(qkv_b[g], qkv_b[B_GROUPS + g], qkv_b[2 * B_GROUPS + g], bias_tbl[g], g)
                        for g in range(B_GROUPS)])
        conv_w8 = jnp.pad(conv_w[l], ((0, 8 - C_CONV), (0, 0)))
        xact = _conv_silu(xbc, conv_w8, conv_b[l].reshape(1, C_XBC))
        prm = jnp.stack([_pad_row(dt_bias[l].reshape(-1)), _pad_row(a_log[l].reshape(-1)),
                         _pad_row(jnp.tile(d_skip[l], 2))] + [jnp.zeros((LANES,), F32)] * 5, axis=0)
        yf, yb = _ssd(xact, dt, prm)
        perm_rows = lambda w: w.reshape(A_HEADS, HEAD_DIM, d)[A_HEAD_ORDER, :, :].reshape(A_Q, d)
        x = _merge(x, gate, oa, ga, ob, lse, gb, yf, yb, zc, mg,
                   b_gate[l].reshape(1, -1), ssm_norm_w[l].reshape(1, -1),
                   perm_rows(w_proj_a[l]).astype(BF16), w_proj_b[l].astype(BF16),
                   w_proj_c[l].astype(BF16), w_out[l].astype(BF16))
    return x
```

```python
import functools
import math

import numpy as np
import jax
import jax.numpy as jnp
from jax import lax
from jax.experimental import pallas as pl
from jax.experimental.pallas import tpu as pltpu

F32 = jnp.float32
BF16 = jnp.bfloat16

D_MODEL = 1024
HEAD_DIM = 64
EPS = 1e-6
NEG_INF = -1e30
GRID_W = 64
A_HEADS = 8
A_KV_HEADS = 2
ROPE_THETA = 10000.0
B_GROUPS = 3
B_HEADS_PER_GROUP = 4
B_WINDOWS = (128, 512, 2048)
B_DILATIONS = (1, 4, 16)
N_BUCKETS = 32
MAX_DISTANCE = 1024
C_HEADS = 8
C_HEAD_DIM = 64
C_INNER = C_HEADS * C_HEAD_DIM
C_GROUPS = 2
C_STATE = 128
C_CONV = 5
C_CHUNK = 128
N_BRANCH = 3

A_Q = A_HEADS * HEAD_DIM
A_KV = A_KV_HEADS * HEAD_DIM
B_QKV = B_GROUPS * B_HEADS_PER_GROUP * HEAD_DIM
B_OUT = B_HEADS_PER_GROUP * HEAD_DIM
C_BC = C_GROUPS * C_STATE
C_XBC = C_INNER + 2 * C_BC
SPLIT_WIDTHS = (A_Q, A_KV, A_KV, A_Q, B_QKV, B_QKV, B_QKV, B_OUT,
                C_INNER, C_INNER, C_BC, C_BC, C_HEADS, C_HEADS, N_BRANCH * D_MODEL)
SPLIT_OFFSETS = tuple(int(v) for v in np.cumsum((0,) + SPLIT_WIDTHS))

LANES = 128
VMEM_LIMIT = 56 * 1024 * 1024
LOG2E = math.log2(math.e)

_W_GROUPS = (("qa", A_Q), ("ka", A_KV), ("dt", LANES), ("ga", A_Q), ("qb", B_QKV), ("kb", B_QKV), ("vb", B_QKV),
             ("gb", B_OUT), ("xbc", C_XBC), ("zc", C_INNER), ("mg", N_BRANCH * D_MODEL))
MXU_COLS = 256
_W_OFF = {}
_o = 0
for _n, _w in _W_GROUPS:
    _W_OFF[_n] = (_o, _w)
    _o += _w
W_MAIN_COLS = _o

A_HEAD_ORDER = tuple(h for j in range(4) for h in (j, j + 4))


def _cparams(sem):
    return pltpu.CompilerParams(dimension_semantics=sem, vmem_limit_bytes=VMEM_LIMIT)


def _silu(x):
    return x * jax.nn.sigmoid(x)


def _mod_kernel(c_ref, w_ref, b_ref, o_ref):
    c = c_ref[...]
    o_ref[0] = jnp.dot(_silu(c), w_ref[0], preferred_element_type=F32,
                       precision=lax.Precision.HIGHEST) + b_ref[0]


def _modulation(c_pad, w_ada, b_ada):
    depth, d, n = w_ada.shape
    tn = 512
    return pl.pallas_call(
        _mod_kernel,
        grid=(depth, n // tn),
        in_specs=[pl.BlockSpec((c_pad.shape[0], d), lambda l, j: (0, 0)),
                  pl.BlockSpec((1, d, tn), lambda l, j: (l, 0, j)),
                  pl.BlockSpec((1, 1, tn), lambda l, j: (l, 0, j))],
        out_specs=pl.BlockSpec((1, c_pad.shape[0], tn), lambda l, j: (l, 0, j)),
        out_shape=jax.ShapeDtypeStruct((depth, c_pad.shape[0], n), F32),
        compiler_params=_cparams(("arbitrary", "arbitrary")),
        name="adaln_mod",
    )(c_pad, w_ada, b_ada.reshape(depth, 1, n))


def _head_sumsq_matrix():
    r = lax.broadcasted_iota(jnp.int32, (MXU_COLS, MXU_COLS), 0) // HEAD_DIM
    c = lax.broadcasted_iota(jnp.int32, (MXU_COLS, MXU_COLS), 1) // HEAD_DIM
    return jnp.where(r == c, 1.0 / HEAD_DIM, 0.0).astype(BF16)


def _inproj_kernel(x_ref, sc_ref, sh_ref, nw_ref, cos_ref, sin_ref, hw_ref, bg_ref, w_ref, wt_ref,
                   qa_ref, ka_ref, vt_ref, ga_ref,
                   qb0_ref, qb1_ref, qb2_ref, kb0_ref, kb1_ref, kb2_ref, vb0_ref, vb1_ref, vb2_ref,
                   gb_ref, xbc_ref, zc_ref, dt_ref, mg_ref, h_ref):
    x = x_ref[0]
    tm = x.shape[0]
    ms = jnp.mean(x * x, axis=-1, keepdims=True)
    h = (x * lax.rsqrt(ms + EPS)) * nw_ref[...]
    h = h * sc_ref[0] + sh_ref[0]
    hb = h.astype(BF16)
    for j in range(D_MODEL // LANES):
        h_ref[j] = h[:, j * LANES:(j + 1) * LANES]

    bd = _head_sumsq_matrix()
    lane = lax.broadcasted_iota(jnp.int32, (1, LANES), 1)
    first_half = (lane % 32) < 16
    cos = cos_ref[...]
    sin = sin_ref[...]

    def proj(name, c0, c1):
        off, _ = _W_OFF[name]
        return jnp.dot(hb, w_ref[:, off + c0:off + c1], preferred_element_type=F32)

    def head_norm(y, w_row):
        n = y.shape[1]
        ss = jnp.dot((y * y).astype(BF16), bd[:n, :n], preferred_element_type=F32)
        return (y * lax.rsqrt(ss + EPS)) * jnp.concatenate([w_row] * (n // LANES), axis=1)

    def rope(y):
        sw = jnp.where(first_half, pltpu.roll(y, LANES - 16, 1), pltpu.roll(y, 16, 1))
        return y * cos + sw * sin

    deferred = []

    def normed(y, finish):
        waiting = list(deferred)
        deferred[:] = [(y, finish)]
        for py, pfinish in waiting:
            pfinish(py)

    def finish_qa(y, c0):
        y = head_norm(y, hw_ref[0:1, :])
        for j in range(0, y.shape[1], LANES):
            blk = rope(y[:, j:j + LANES]) * (HEAD_DIM ** -0.5 * LOG2E)
            qa_ref[0, :, c0 + j:c0 + j + LANES] = blk.astype(BF16)

    def finish_ka(y):
        ka_ref[0] = rope(head_norm(y, hw_ref[1:2, :])).astype(BF16)

    for c0 in range(0, A_Q, MXU_COLS):
        normed(proj("qa", c0, c0 + MXU_COLS), functools.partial(finish_qa, c0=c0))
    ka_dt = proj("ka", 0, A_KV + LANES)
    dt_ref[0] = ka_dt[:, A_KV:]
    normed(ka_dt[:, :A_KV], finish_ka)
    vt_ref[0] = lax.dot_general(wt_ref[...], hb, (((1,), (1,)), ((), ())),
                                preferred_element_type=F32).astype(BF16)
    for j in range(0, A_Q, 256):
        ga_ref[0, :, j:j + 256] = _silu(proj("ga", j, j + 256)).astype(BF16)
    def store_phases(ref, lanes, y, d):
        n = tm // d
        for r in range(d):
            ref[0, r, :, lanes] = y[r * n:(r + 1) * n, :]

    for g, (qr, kr, vr) in enumerate(((qb0_ref, kb0_ref, vb0_ref), (qb1_ref, kb1_ref, vb1_ref),
                                      (qb2_ref, kb2_ref, vb2_ref))):
        d = B_DILATIONS[g]
        if d == 1:
            lhs = hb
        else:
            lhs = jnp.concatenate(
                [jnp.concatenate([h_ref[j, pl.ds(r, tm // d, stride=d), :] for r in range(d)], axis=0)
                 for j in range(D_MODEL // LANES)], axis=1).astype(BF16)

        def proj_g(name, c0, c1, lhs=lhs):
            off, _ = _W_OFF[name]
            return jnp.dot(lhs, w_ref[:, off + c0:off + c1], preferred_element_type=F32)

        def finish_q(y, ref=qr, d=d):
            store_phases(ref, slice(0, B_OUT), (head_norm(y, hw_ref[2:3, :]) * (HEAD_DIM ** -0.5)).astype(BF16), d)

        def finish_k(y, ref=kr, d=d):
            store_phases(ref, slice(0, B_OUT), head_norm(y, hw_ref[3:4, :]).astype(BF16), d)

        normed(proj_g("qb", g * B_OUT, (g + 1) * B_OUT), finish_q)
        normed(proj_g("kb", g * B_OUT, (g + 1) * B_OUT), finish_k)
        v = proj_g("vb", g * B_OUT, (g + 1) * B_OUT)
        store_phases(vr, slice(0, B_OUT), v.astype(BF16), d)
    normed(None, None)
    deferred.clear()
    gb_ref[0] = _silu(proj("gb", 0, B_OUT)).astype(BF16)
    for j in range(0, C_XBC, 256):
        xbc_ref[0, :, j:j + 256] = proj("xbc", j, j + 256).astype(BF16)
    for j in range(0, C_INNER, 256):
        zc_ref[0, :, j:j + 256] = _silu(proj("zc", j, j + 256)).astype(BF16)
    for j in range(0, N_BRANCH * D_MODEL, 512):
        mg_ref[0, :, j:j + 512] = jax.nn.sigmoid(proj("mg", j, j + 512) + bg_ref[:, j:j + 512]).astype(BF16)


def _in_projection(x, sc1p, shift, norm_w, cos_t, sin_t, head_w, b_gate, w_main, w_t, tm=256):
    bsz, s, d = x.shape
    nt = s // tm
    row = lambda i, w: pl.BlockSpec((1, tm, w), lambda b, i: (b, i, 0))
    const2 = lambda shp: pl.BlockSpec(shp, lambda b, i: (0, 0))
    phase = [("phase", dil) for _ in range(3) for dil in B_DILATIONS]
    outs = ([("qa", A_Q, BF16), ("ka", A_KV, BF16), None, ("ga", A_Q, BF16)] + phase +
            [("gb", B_OUT, BF16), ("xbc", C_XBC, BF16),
             ("zc", C_INNER, BF16), ("dt", LANES, F32), ("mg", N_BRANCH * D_MODEL, BF16)])
    out_shape, out_specs = [], []
    for o in outs:
        if o is None:
            out_shape.append(jax.ShapeDtypeStruct((bsz, A_KV, s), BF16))
            out_specs.append(pl.BlockSpec((1, A_KV, tm), lambda b, i: (b, 0, i)))
        elif o[0] == "phase":
            out_shape.append(jax.ShapeDtypeStruct((bsz, o[1], s // o[1], B_OUT), BF16))
            out_specs.append(pl.BlockSpec((1, o[1], tm // o[1], B_OUT), lambda b, i: (b, 0, i, 0)))
        else:
            out_shape.append(jax.ShapeDtypeStruct((bsz, s, o[1]), o[2]))
            out_specs.append(row(0, o[1]))
    return pl.pallas_call(
        _inproj_kernel,
        grid=(bsz, nt),
        in_specs=[row(0, d),
                  pl.BlockSpec((1, 1, d), lambda b, i: (b, 0, 0)),
                  pl.BlockSpec((1, 1, d), lambda b, i: (b, 0, 0)),
                  const2((1, d)),
                  pl.BlockSpec((tm, LANES), lambda b, i: (i, 0)),
                  pl.BlockSpec((tm, LANES), lambda b, i: (i, 0)),
                  const2((8, LANES)),
                  const2(b_gate.shape),
                  const2(w_main.shape),
                  const2(w_t.shape)],
        out_specs=out_specs,
        out_shape=out_shape,
        scratch_shapes=[pltpu.VMEM((d // LANES, tm, LANES), F32)],
        compiler_params=_cparams(("arbitrary", "arbitrary")),
        name="norm_inproj",
    )(x, sc1p, shift, norm_w, cos_t, sin_t, head_w, b_gate, w_main, w_t)


A_AUG = HEAD_DIM + 8


def _attn_a_kernel(bounded_ref, q_ref, k_ref, vt_ref, o_ref, qt_ref, m_ref, acc_ref, st_ref, *, kc):
    tq = q_ref.shape[1]
    s_len = k_ref.shape[1]
    n_blk = A_Q // LANES
    lane = lax.broadcasted_iota(jnp.int32, (tq, LANES), 1)
    for j in range(n_blk):
        qblk = q_ref[0, :, j * LANES:(j + 1) * LANES]
        for kv in range(A_KV_HEADS):
            in_half = (lane < HEAD_DIM) if kv == 0 else (lane >= HEAD_DIM)
            qt_ref[A_KV_HEADS * j + kv] = jnp.where(in_half, qblk, jnp.zeros_like(qblk))
    m_ref[...] = jnp.full(m_ref.shape, NEG_INF, F32)
    acc_ref[...] = jnp.zeros(acc_ref.shape, F32)
    ones = jnp.ones((A_AUG - HEAD_DIM, kc), BF16)

    n_chunks = s_len // kc

    def scores(c, h):
        off = pl.multiple_of(c * kc, kc)
        return lax.dot_general(k_ref[0, pl.ds(off, kc), :], qt_ref[h], (((1,), (1,)), ((), ())),
                               preferred_element_type=F32)

    ahead = st_ref.shape[0]

    def sweep(online, unroll):
        for i in range(ahead):
            st_ref[i] = scores(0, i)

        def vaug(c, kv):
            off = pl.multiple_of(c * kc, kc)
            return jnp.concatenate([vt_ref[0, kv * HEAD_DIM:(kv + 1) * HEAD_DIM, pl.ds(off, kc)], ones], axis=0)

        def body(cc, carry):
            pending = [st_ref[i] for i in range(ahead)]
            for t in range(unroll * A_HEADS):
                c, h = cc * unroll + t // A_HEADS, t % A_HEADS
                nxt = t + ahead
                c_nxt = jnp.minimum(cc * unroll + nxt // A_HEADS, n_chunks - 1)
                pending.append(scores(c_nxt, nxt % A_HEADS))
                st = pending.pop(0)
                v = vaug(c, h % A_KV_HEADS)
                if online:
                    m = m_ref[h:h + 1, :]
                    m_new = jnp.maximum(m, jnp.max(st, axis=0, keepdims=True))
                    alpha = jnp.exp2(m - m_new)
                    p = jnp.exp2(st - m_new).astype(BF16)
                    acc_ref[h] = alpha * acc_ref[h] + jnp.dot(v, p, preferred_element_type=F32)
                    m_ref[h:h + 1, :] = m_new
                else:
                    p = jnp.exp2(st).astype(BF16)
                    acc_ref[h] += jnp.dot(v, p, preferred_element_type=F32)
            for i in range(ahead):
                st_ref[i] = pending[i]
            return carry

        lax.fori_loop(0, n_chunks // unroll, body, 0)

    bounded = bounded_ref[0] == 1
    pl.when(bounded)(lambda: sweep(False, 4))
    pl.when(jnp.logical_not(bounded))(lambda: sweep(True, 1))

    for j in range(n_blk):
        halves = []
        for kv in range(A_KV_HEADS):
            a = acc_ref[A_KV_HEADS * j + kv]
            halves.append(a[:HEAD_DIM] / a[HEAD_DIM:HEAD_DIM + 1])
        o_ref[0, :, j * LANES:(j + 1) * LANES] = jnp.concatenate(halves, axis=0).T.astype(o_ref.dtype)


A_SCORE_LIMIT = 40.0


def _score_bound(q_norm, k_norm):
    return (1.05 * HEAD_DIM * HEAD_DIM ** -0.5 * LOG2E) * jnp.max(jnp.abs(q_norm)) * jnp.max(jnp.abs(k_norm))


def _mixer_a(qa, ka, vt, bound, tq=256, kc=512):
    bsz, s, _ = qa.shape
    bounded = (bound <= A_SCORE_LIMIT).astype(jnp.int32).reshape(1)
    return pl.pallas_call(
        functools.partial(_attn_a_kernel, kc=kc),
        grid=(bsz, s // tq),
        in_specs=[pl.BlockSpec(memory_space=pltpu.SMEM),
                  pl.BlockSpec((1, tq, A_Q), lambda b, i: (b, i, 0)),
                  pl.BlockSpec((1, s, A_KV), lambda b, i: (b, 0, 0)),
                  pl.BlockSpec((1, A_KV, s), lambda b, i: (b, 0, 0))],
        out_specs=pl.BlockSpec((1, tq, A_Q), lambda b, i: (b, i, 0)),
        out_shape=jax.ShapeDtypeStruct((bsz, s, A_Q), BF16),
        scratch_shapes=[pltpu.VMEM((A_HEADS, tq, LANES), BF16),
                        pltpu.VMEM((A_HEADS, tq), F32),
                        pltpu.VMEM((A_HEADS, A_AUG, tq), F32),
                        pltpu.VMEM((3, kc, tq), F32)],
        compiler_params=_cparams(("arbitrary", "arbitrary")),
        name="mixer_a_attn",
    )(bounded, qa, ka, vt)


B_SIDE = 64
B_TQ = 256
B_BAND = B_TQ + 2 * B_SIDE
B_STEP = 2 * B_TQ


def _attn_b_kernel(q_ref, kp_ref, kc_ref, kn_ref, vp_ref, vc_ref, vn_ref, bias_ref, o_ref, lse_ref):
    mi = pl.program_id(2)
    n_m = pl.num_programs(2)
    n_sub = q_ref.shape[2] // B_TQ
    kall = jnp.concatenate([kp_ref[0, 0], kc_ref[0, 0], kn_ref[0, 0]], axis=0)
    vall = jnp.concatenate([vp_ref[0, 0], vc_ref[0, 0], vn_ref[0, 0]], axis=0)
    kk = lax.broadcasted_iota(jnp.int32, (1, B_BAND), 1)
    lane = lax.broadcasted_iota(jnp.int32, (B_TQ, B_OUT), 1) // HEAD_DIM
    for t in range(n_sub):
        q = q_ref[0, 0, t * B_TQ:(t + 1) * B_TQ, :]
        kband = kall[t * B_TQ:t * B_TQ + B_BAND]
        vband = vall[t * B_TQ:t * B_TQ + B_BAND]
        key_ok = None
        if t == 0:
            key_ok = jnp.logical_or(kk >= B_SIDE, mi > 0)
        if t == n_sub - 1:
            hi_ok = jnp.logical_or(kk < B_SIDE + B_TQ, mi < n_m - 1)
            key_ok = hi_ok if key_ok is None else jnp.logical_and(key_ok, hi_ok)
        o_acc = jnp.zeros((B_TQ, B_OUT), F32)
        lse_acc = jnp.zeros((B_TQ, B_OUT), F32)
        scores = [lax.dot_general(jnp.where(lane == h, q, jnp.zeros_like(q)), kband, (((1,), (1,)), ((), ())),
                                  preferred_element_type=F32) for h in range(B_HEADS_PER_GROUP)]
        for h in range(B_HEADS_PER_GROUP):
            s = scores[h] + bias_ref[h]
            if key_ok is not None:
                s = jnp.where(key_ok, s, NEG_INF)
            mx = jnp.max(s, axis=-1, keepdims=True)
            e = jnp.exp(s - mx)
            den = jnp.sum(e, axis=-1, keepdims=True)
            p = (e / den).astype(BF16)
            oh = jnp.dot(p, vband, preferred_element_type=F32)
            o_acc = jnp.where(lane == h, oh, o_acc)
            lse_acc = jnp.where(lane == h, mx + jnp.log(den), lse_acc)
        o_ref[0, 0, t * B_TQ:(t + 1) * B_TQ, :] = o_acc.astype(o_ref.dtype)
        lse_ref[0, 0, t * B_TQ:(t + 1) * B_TQ, :] = lse_acc


def _mixer_b_group(qb, kb, vb, bias_g, g):
    bsz, d, m, _ = qb.shape
    assert m % B_STEP == 0, (m, B_STEP)
    n_m = m // B_STEP
    hb = B_STEP // B_SIDE
    n_side = m // B_SIDE
    cur = pl.BlockSpec((1, 1, B_STEP, B_OUT), lambda b, r, i: (b, r, i, 0))
    prev = pl.BlockSpec((1, 1, B_SIDE, B_OUT), lambda b, r, i: (b, r, jnp.maximum(i * hb - 1, 0), 0))
    nxt = pl.BlockSpec((1, 1, B_SIDE, B_OUT),
                       lambda b, r, i: (b, r, jnp.minimum((i + 1) * hb, n_side - 1), 0))
    return pl.pallas_call(
        _attn_b_kernel,
        grid=(bsz, d, n_m),
        in_specs=[cur, prev, cur, nxt, prev, cur, nxt,
                  pl.BlockSpec((B_HEADS_PER_GROUP, B_TQ, B_BAND), lambda b, r, i: (0, 0, 0))],
        out_specs=[cur, cur],
        out_shape=[jax.ShapeDtypeStruct((bsz, d, m, B_OUT), F32)] * 2,
        compiler_params=_cparams(("arbitrary", "arbitrary", "arbitrary")),
        name=f"mixer_b_attn_g{g}",
    )(qb, kb, kb, kb, vb, vb, vb, bias_g)


def _t5_bucket(rel):
    nb = N_BUCKETS // 2
    max_exact = nb // 2
    ret = jnp.where(rel > 0, nb, 0)
    n = jnp.abs(rel)
    nf = jnp.maximum(n, 1).astype(F32)
    large = max_exact + (jnp.log(nf / max_exact) / math.log(MAX_DISTANCE / max_exact)
                         * (nb - max_exact)).astype(jnp.int32)
    large = jnp.minimum(large, nb - 1)
    return ret + jnp.where(n < max_exact, n, large)


def _band_bias(rel_bias):
    rel = jnp.arange(-B_SIDE, B_SIDE + 1)
    n_rel = 2 * B_SIDE + 1
    width = B_TQ + B_BAND
    nh = B_HEADS_PER_GROUP
    tables = []
    for g in range(B_GROUPS):
        tbl = rel_bias[:, g * nh:(g + 1) * nh]
        b1 = tbl[_t5_bucket(rel * B_DILATIONS[g])].T.astype(F32)
        w = jnp.concatenate([b1, jnp.full((nh, width - n_rel), NEG_INF, F32)], axis=1)
        flat = jnp.broadcast_to(w[:, None, :], (nh, B_TQ, width)).reshape(nh, B_TQ * width)
        tables.append(flat[:, :B_TQ * (width - 1)].reshape(nh, B_TQ, width - 1)[:, :, :B_BAND])
    return jnp.stack(tables, axis=0)


CONV_HALO = 8


def _conv_kernel(xp_ref, xc_ref, xn_ref, w_ref, b_ref, o_ref, ext_ref):
    i = pl.program_id(1)
    n_t = pl.num_programs(1)
    tc = xc_ref.shape[1]
    prev = jnp.where(i > 0, xp_ref[0].astype(F32), 0.0)
    nxt = jnp.where(i < n_t - 1, xn_ref[0].astype(F32), 0.0)
    ext_ref[0:CONV_HALO, :] = prev
    ext_ref[CONV_HALO:CONV_HALO + tc, :] = xc_ref[0].astype(F32)
    ext_ref[CONV_HALO + tc:, :] = nxt
    acc = jnp.zeros((tc, xc_ref.shape[2]), F32) + b_ref[...]
    for k in range(C_CONV):
        start = CONV_HALO + k - C_CONV // 2
        acc = acc + ext_ref[start:start + tc, :] * w_ref[k:k + 1, :]
    o_ref[0] = _silu(acc).astype(o_ref.dtype)


def _conv_silu(xbc, conv_w, conv_b, tc=512):
    bsz, s, ch = xbc.shape
    n_t = s // tc
    hb = tc // CONV_HALO
    n_h = s // CONV_HALO
    return pl.pallas_call(
        _conv_kernel,
        grid=(bsz, n_t),
        in_specs=[pl.BlockSpec((1, CONV_HALO, ch), lambda b, i: (b, jnp.maximum(i * hb - 1, 0), 0)),
                  pl.BlockSpec((1, tc, ch), lambda b, i: (b, i, 0)),
                  pl.BlockSpec((1, CONV_HALO, ch), lambda b, i: (b, jnp.minimum((i + 1) * hb, n_h - 1), 0)),
                  pl.BlockSpec((8, ch), lambda b, i: (0, 0)),
                  pl.BlockSpec((1, ch), lambda b, i: (0, 0))],
        out_specs=pl.BlockSpec((1, tc, ch), lambda b, i: (b, i, 0)),
        out_shape=jax.ShapeDtypeStruct((bsz, s, ch), BF16),
        scratch_shapes=[pltpu.VMEM((tc + 2 * CONV_HALO, ch), F32)],
        compiler_params=_cparams(("arbitrary", "arbitrary")),
        name="mixer_c_conv",
    )(xbc, xbc, xbc, conv_w, conv_b)


def _split3(x):
    hi = x.astype(BF16)
    r1 = x - hi.astype(F32)
    mid = r1.astype(BF16)
    lo = (r1 - mid.astype(F32)).astype(BF16)
    return hi, mid, lo


def _ssd_direction(x_ref, dt_ref, prm_ref, state_ref, y_ref, *, reverse):
    L = C_CHUNK
    col0 = C_HEADS if reverse else 0
    r = lax.broadcasted_iota(jnp.int32, (L, L), 0)
    c = lax.broadcasted_iota(jnp.int32, (L, L), 1)
    causal = (r <= c) if reverse else (r >= c)
    tri = jnp.where(causal, 1.0, 0.0).astype(BF16)

    dt_bias = prm_ref[0:1, :]
    a_neg = -jnp.exp(prm_ref[1:2, :])
    dtv = jax.nn.softplus(dt_ref[0] + dt_bias)
    a = dtv * a_neg
    hi, mid, lo = _split3(a)
    cs = (jnp.dot(tri, hi, preferred_element_type=F32) + jnp.dot(tri, mid, preferred_element_type=F32)
          + jnp.dot(tri, lo, preferred_element_type=F32))
    cs_t = cs.T
    tot = cs[0:1, :] if reverse else cs[L - 1:L, :]
    decay_in = jnp.exp(cs)
    decay_out = jnp.exp(tot - cs)
    chunk_decay = jnp.exp(tot)

    lane_head = lax.broadcasted_iota(jnp.int32, (1, 4 * C_HEAD_DIM), 1) // C_HEAD_DIM

    def expand_row(m, g):
        out = jnp.zeros((1, 4 * C_HEAD_DIM), F32)
        for j in range(4):
            cj = col0 + g * 4 + j
            out = jnp.where(lane_head == j, m[:, cj:cj + 1], out)
        return out

    sel = (lax.broadcasted_iota(jnp.int32, (LANES, C_INNER), 0)
           == col0 + lax.broadcasted_iota(jnp.int32, (LANES, C_INNER), 1) // C_HEAD_DIM)
    sel = jnp.where(sel, 1.0, 0.0).astype(BF16)

    def expand(m):
        hi = m.astype(BF16)
        lo = (m - hi.astype(F32)).astype(BF16)
        return jnp.dot(hi, sel, preferred_element_type=F32) + jnp.dot(lo, sel, preferred_element_type=F32)

    dt_e, din_e, dout_e = expand(dtv), expand(decay_in), expand(decay_out)

    for g in range(C_GROUPS):
        lanes = slice(g * 256, (g + 1) * 256)
        xg = x_ref[0, :, lanes].astype(F32)
        bm = x_ref[0, :, C_INNER + g * C_STATE:C_INNER + (g + 1) * C_STATE]
        cm = x_ref[0, :, C_INNER + C_BC + g * C_STATE:C_INNER + C_BC + (g + 1) * C_STATE]
        cb = lax.dot_general(cm, bm, (((1,), (1,)), ((), ())), preferred_element_type=F32)
        xdt = xg * dt_e[:, lanes]
        state = state_ref[g]
        y = jnp.dot(cm, state.astype(BF16), preferred_element_type=F32) * din_e[:, lanes]
        xdt_b = xdt.astype(BF16)
        parts = []
        for j in range(4):
            cj = col0 + g * 4 + j
            diff = cs[:, cj:cj + 1] - cs_t[cj:cj + 1, :]
            lm = jnp.exp(jnp.where(causal, diff, NEG_INF))
            gmat = (cb * lm).astype(BF16)
            parts.append(jnp.dot(gmat, xdt_b[:, j * C_HEAD_DIM:(j + 1) * C_HEAD_DIM],
                                 preferred_element_type=F32))
        y = y + jnp.concatenate(parts, axis=1)
        if not reverse:
            y = y + xg * expand_row(prm_ref[2:3, :], g)
        y_ref[0, :, lanes] = y
        xs = (xdt * dout_e[:, lanes]).astype(BF16)
        upd = lax.dot_general(bm, xs, (((0,), (0,)), ((), ())), preferred_element_type=F32)
        state_ref[g] = state * expand_row(chunk_decay, g) + upd


def _ssd_kernel(xf_ref, xb_ref, dtf_ref, dtb_ref, prm_ref, yf_ref, yb_ref, sf_ref, sb_ref):
    @pl.when(pl.program_id(1) == 0)
    def _():
        sf_ref[...] = jnp.zeros_like(sf_ref)
        sb_ref[...] = jnp.zeros_like(sb_ref)

    _ssd_direction(xf_ref, dtf_ref, prm_ref, sf_ref, yf_ref, reverse=False)
    _ssd_direction(xb_ref, dtb_ref, prm_ref, sb_ref, yb_ref, reverse=True)


def _ssd(xact, dt, prm):
    bsz, s, ch = xact.shape
    n_c = s // C_CHUNK
    fwd = lambda w: pl.BlockSpec((1, C_CHUNK, w), lambda b, i: (b, i, 0))
    bwd = lambda w: pl.BlockSpec((1, C_CHUNK, w), lambda b, i: (b, n_c - 1 - i, 0))
    return pl.pallas_call(
        _ssd_kernel,
        grid=(bsz, n_c),
        in_specs=[fwd(ch), bwd(ch), fwd(LANES), bwd(LANES), pl.BlockSpec((8, LANES), lambda b, i: (0, 0))],
        out_specs=[fwd(C_INNER), bwd(C_INNER)],
        out_shape=[jax.ShapeDtypeStruct((bsz, s, C_INNER), F32)] * 2,
        scratch_shapes=[pltpu.VMEM((C_GROUPS, C_STATE, 4 * C_HEAD_DIM), F32)] * 2,
        compiler_params=_cparams(("arbitrary", "arbitrary")),
        name="mixer_c_ssd",
    )(xact, xact, dt, dt, prm)


def _merge_kernel(x_ref, gate_ref, oa_ref, ga_ref, ob0_ref, ob1_ref, ob2_ref, ls0_ref, ls1_ref, ls2_ref,
                  gb_ref, yf_ref, yb_ref, zc_ref, mg_ref, nw_ref, wa_ref, wb_ref, wc_ref, wo_ref,
                  o_ref, il_ref):
    ya = (oa_ref[0].astype(F32) * ga_ref[0].astype(F32)).astype(BF16)

    def token_order(ref, slot):
        d, n = ref.shape[1], ref.shape[2]
        if d == 1:
            return ref[0, 0]
        halves = B_OUT // LANES
        for r in range(d):
            for j in range(halves):
                il_ref[slot * halves + j, pl.ds(r, n, stride=d), :] = ref[0, r, :, j * LANES:(j + 1) * LANES]
        return jnp.concatenate([il_ref[slot * halves + j] for j in range(halves)], axis=1)

    l0, l1, l2 = token_order(ls0_ref, 0), token_order(ls1_ref, 0), token_order(ls2_ref, 1)
    o0, o1, o2 = token_order(ob0_ref, 0), token_order(ob1_ref, 2), token_order(ob2_ref, 3)
    mx = jnp.maximum(jnp.maximum(l0, l1), l2)
    e0, e1, e2 = jnp.exp(l0 - mx), jnp.exp(l1 - mx), jnp.exp(l2 - mx)
    yb = (e0 * o0 + e1 * o1 + e2 * o2) / (e0 + e1 + e2)
    yb = (yb * gb_ref[0].astype(F32)).astype(BF16)
    y = (yf_ref[0] + yb_ref[0]) * zc_ref[0].astype(F32)
    yc = ((y * lax.rsqrt(jnp.mean(y * y, axis=-1, keepdims=True) + EPS)) * nw_ref[...]).astype(BF16)

    d = D_MODEL
    gates = lambda k: mg_ref[0, :, k * d:(k + 1) * d].astype(F32)
    merged = gates(0) * jnp.dot(ya, wa_ref[...], preferred_element_type=F32)
    merged = merged + gates(1) * jnp.dot(yb, wb_ref[...], preferred_element_type=F32)
    merged = merged + gates(2) * jnp.dot(yc, wc_ref[...], preferred_element_type=F32)
    out = jnp.dot(merged.astype(BF16), wo_ref[...], preferred_element_type=F32)
    o_ref[0] = x_ref[0] + gate_ref[0] * out


def _merge(x, gate, oa, ga, ob, lse, gb, yf, yb, zc, mg, ssm_w, wa, wb, wc, wo, tm=256):
    bsz, s, d = x.shape
    row = lambda w: pl.BlockSpec((1, tm, w), lambda b, i: (b, i, 0))
    phase = [pl.BlockSpec((1, dil, tm // dil, B_OUT), lambda b, i: (b, 0, i, 0)) for dil in B_DILATIONS]
    const2 = lambda shp: pl.BlockSpec(shp, lambda b, i: (0, 0))
    return pl.pallas_call(
        _merge_kernel,
        grid=(bsz, s // tm),
        in_specs=[row(d), pl.BlockSpec((1, 1, d), lambda b, i: (b, 0, 0)),
                  row(A_Q), row(A_Q)] + phase + phase +
                 [row(B_OUT), row(C_INNER), row(C_INNER), row(C_INNER),
                  row(N_BRANCH * d), const2((1, C_INNER)),
                  const2(wa.shape), const2(wb.shape), const2(wc.shape), const2(wo.shape)],
        out_specs=row(d),
        out_shape=jax.ShapeDtypeStruct((bsz, s, d), F32),
        scratch_shapes=[pltpu.VMEM((4 * B_OUT // LANES, tm, LANES), F32)],
        compiler_params=_cparams(("arbitrary", "arbitrary")),
        name="merge_out",
    )(x, gate, oa, ga, *ob, *lse, gb, yf, yb, zc, mg, ssm_w, wa, wb, wc, wo)


def _rope_tables(s):
    quarter = HEAD_DIM // 4
    freqs = ROPE_THETA ** (-jnp.arange(quarter, dtype=F32) / quarter)
    t = jnp.arange(s)
    ang_r = (t // GRID_W).astype(F32)[:, None] * freqs
    ang_c = (t % GRID_W).astype(F32)[:, None] * freqs
    cos = jnp.concatenate([jnp.cos(ang_r)] * 2 + [jnp.cos(ang_c)] * 2, axis=-1)
    sin = jnp.concatenate([-jnp.sin(ang_r), jnp.sin(ang_r), -jnp.sin(ang_c), jnp.sin(ang_c)], axis=-1)
    return jnp.tile(cos, (1, LANES // HEAD_DIM)), jnp.tile(sin, (1, LANES // HEAD_DIM))


def _layer_weights(w_in_l):
    cols = lambda k: w_in_l[:, SPLIT_OFFSETS[k]:SPLIT_OFFSETS[k + 1]]
    d = w_in_l.shape[0]
    perm = lambda w: w.reshape(d, A_HEADS, HEAD_DIM)[:, A_HEAD_ORDER, :].reshape(d, A_Q)
    dt_cols = jnp.concatenate([cols(12), cols(13), jnp.zeros((d, LANES - 2 * C_HEADS), w_in_l.dtype)], axis=1)
    pieces = {"qa": perm(cols(0)), "ka": cols(1), "ga": perm(cols(3)), "qb": cols(4), "kb": cols(5),
              "vb": cols(6), "gb": cols(7), "xbc": jnp.concatenate([cols(8), cols(10), cols(11)], axis=1),
              "zc": cols(9), "dt": dt_cols, "mg": cols(14)}
    w_main = jnp.concatenate([pieces[n] for n, _ in _W_GROUPS], axis=1).astype(BF16)
    w_t = cols(2).T.astype(BF16)
    return w_main, w_t


def _pad_row(v, width=LANES):
    return jnp.pad(v.astype(F32), (0, width - v.shape[0]))


def kernel(x, c, norm_w, w_ada, b_ada, w_in, b_gate, q_norm_a, k_norm_a, q_norm_b, k_norm_b, rel_bias,
           conv_w, conv_b, a_log, dt_bias, d_skip, ssm_norm_w, w_proj_a, w_proj_b, w_proj_c, w_out):
    bsz, s, d = x.shape
    depth = w_in.shape[0]
    c_pad = jnp.pad(c, ((0, 8 - bsz), (0, 0)))
    mod = _modulation(c_pad, w_ada, b_ada)[:, :bsz]
    cos_t, sin_t = _rope_tables(s)
    bias_tbl = _band_bias(rel_bias)
    tile2 = lambda v: jnp.tile(v.astype(F32), LANES // HEAD_DIM)
    for l in range(depth):
        shift, scale, gate = (mod[l, :, k * d:(k + 1) * d].reshape(bsz, 1, d) for k in range(3))
        w_main, w_t = _layer_weights(w_in[l])
        head_w = jnp.stack([tile2(q_norm_a[l]), tile2(k_norm_a[l]), tile2(q_norm_b[l]), tile2(k_norm_b[l])]
                           + [jnp.zeros((LANES,), F32)] * 4, axis=0)
        (qa, ka, vt, ga, *qkv_b, gb, xbc, zc, dt, mg) = _in_projection(
            x, 1.0 + scale, shift, norm_w[l].reshape(1, d), cos_t, sin_t, head_w,
            b_gate[l].reshape(1, -1), w_main, w_t)
        oa = _mixer_a(qa, ka, vt, _score_bound(q_norm_a[l], k_norm_a[l]))
        ob, lse = zip(*[_mixer_b_group(qkv_b[g], qkv_b[B_GROUPS + g], qkv_b[2 * B_GROUPS + g], bias_tbl[g], g)
                        for g in range(B_GROUPS)])
        conv_w8 = jnp.pad(conv_w[l], ((0, 8 - C_CONV), (0, 0)))
        xact = _conv_silu(xbc, conv_w8, conv_b[l].reshape(1, C_XBC))
        prm = jnp.stack([_pad_row(dt_bias[l].reshape(-1)), _pad_row(a_log[l].reshape(-1)),
                         _pad_row(jnp.tile(d_skip[l], 2))] + [jnp.zeros((LANES,), F32)] * 5, axis=0)
        yf, yb = _ssd(xact, dt, prm)
        perm_rows = lambda w: w.reshape(A_HEADS, HEAD_DIM, d)[A_HEAD_ORDER, :, :].reshape(A_Q, d)
        x = _merge(x, gate, oa, ga, ob, lse, gb, yf, yb, zc, mg, ssm_norm_w[l].reshape(1, -1),
                   perm_rows(w_proj_a[l]).astype(BF16), w_proj_b[l].astype(BF16),
                   w_proj_c[l].astype(BF16), w_out[l].astype(BF16))
    return x
```

```python
import functools
import math

import numpy as np
import jax
import jax.numpy as jnp
from jax import lax
from jax.experimental import pallas as pl
from jax.experimental.pallas import tpu as pltpu

F32 = jnp.float32
BF16 = jnp.bfloat16

D_MODEL = 1024
HEAD_DIM = 64
EPS = 1e-6
NEG_INF = -1e30
GRID_W = 64
A_HEADS = 8
A_KV_HEADS = 2
ROPE_THETA = 10000.0
B_GROUPS = 3
B_HEADS_PER_GROUP = 4
B_WINDOWS = (128, 512, 2048)
B_DILATIONS = (1, 4, 16)
N_BUCKETS = 32
MAX_DISTANCE = 1024
C_HEADS = 8
C_HEAD_DIM = 64
C_INNER = C_HEADS * C_HEAD_DIM
C_GROUPS = 2
C_STATE = 128
C_CONV = 5
C_CHUNK = 128
N_BRANCH = 3

A_Q = A_HEADS * HEAD_DIM
A_KV = A_KV_HEADS * HEAD_DIM
B_QKV = B_GROUPS * B_HEADS_PER_GROUP * HEAD_DIM
B_OUT = B_HEADS_PER_GROUP * HEAD_DIM
C_BC = C_GROUPS * C_STATE
C_XBC = C_INNER + 2 * C_BC
SPLIT_WIDTHS = (A_Q, A_KV, A_KV, A_Q, B_QKV, B_QKV, B_QKV, B_OUT,
                C_INNER, C_INNER, C_BC, C_BC, C_HEADS, C_HEADS, N_BRANCH * D_MODEL)
SPLIT_OFFSETS = tuple(int(v) for v in np.cumsum((0,) + SPLIT_WIDTHS))

LANES = 128
VMEM_LIMIT = 56 * 1024 * 1024
LOG2E = math.log2(math.e)

_W_GROUPS = (("qa", A_Q), ("ka", A_KV), ("dt", LANES), ("ga", A_Q), ("qb", B_QKV), ("kb", B_QKV), ("vb", B_QKV),
             ("gb", B_OUT), ("xbc", C_XBC), ("zc", C_INNER), ("mg", N_BRANCH * D_MODEL))
MXU_COLS = 256
_W_OFF = {}
_o = 0
for _n, _w in _W_GROUPS:
    _W_OFF[_n] = (_o, _w)
    _o += _w
W_MAIN_COLS = _o

A_HEAD_ORDER = tuple(h for j in range(4) for h in (j, j + 4))


def _cparams(sem):
    return pltpu.CompilerParams(dimension_semantics=sem, vmem_limit_bytes=VMEM_LIMIT)


def _silu(x):
    return x * jax.nn.sigmoid(x)


def _mod_kernel(c_ref, w_ref, b_ref, o_ref):
    c = c_ref[...]
    o_ref[0] = jnp.dot(_silu(c), w_ref[0], preferred_element_type=F32,
                       precision=lax.Precision.HIGHEST) + b_ref[0]


def _modulation(c_pad, w_ada, b_ada):
    depth, d, n = w_ada.shape
    tn = 512
    return pl.pallas_call(
        _mod_kernel,
        grid=(depth, n // tn),
        in_specs=[pl.BlockSpec((c_pad.shape[0], d), lambda l, j: (0, 0)),
                  pl.BlockSpec((1, d, tn), lambda l, j: (l, 0, j)),
                  pl.BlockSpec((1, 1, tn), lambda l, j: (l, 0, j))],
        out_specs=pl.BlockSpec((1, c_pad.shape[0], tn), lambda l, j: (l, 0, j)),
        out_shape=jax.ShapeDtypeStruct((depth, c_pad.shape[0], n), F32),
        compiler_params=_cparams(("arbitrary", "arbitrary")),
        name="adaln_mod",
    )(c_pad, w_ada, b_ada.reshape(depth, 1, n))


def _head_sumsq_matrix():
    r = lax.broadcasted_iota(jnp.int32, (MXU_COLS, MXU_COLS), 0) // HEAD_DIM
    c = lax.broadcasted_iota(jnp.int32, (MXU_COLS, MXU_COLS), 1) // HEAD_DIM
    return jnp.where(r == c, 1.0 / HEAD_DIM, 0.0).astype(BF16)


def _inproj_kernel(x_ref, xp_ref, xn_ref, sc_ref, sh_ref, nw_ref, cos_ref, sin_ref, hw_ref, bg_ref,
                   cw_ref, cb_ref, w_ref, wt_ref,
                   qa_ref, ka_ref, vt_ref, ga_ref,
                   qb0_ref, qb1_ref, qb2_ref, kb0_ref, kb1_ref, kb2_ref, vb0_ref, vb1_ref, vb2_ref,
                   gb_ref, xbc_ref, zc_ref, dt_ref, mg_ref, h_ref, ext_ref):
    tm = x_ref.shape[1]

    def modulated(xv):
        ms = jnp.mean(xv * xv, axis=-1, keepdims=True)
        return ((xv * lax.rsqrt(ms + EPS)) * nw_ref[...]) * sc_ref[0] + sh_ref[0]

    h = modulated(x_ref[0])
    hb = h.astype(BF16)
    for j in range(D_MODEL // LANES):
        h_ref[j] = h[:, j * LANES:(j + 1) * LANES]

    bd = _head_sumsq_matrix()
    lane = lax.broadcasted_iota(jnp.int32, (1, LANES), 1)
    first_half = (lane % 32) < 16
    cos = cos_ref[...]
    sin = sin_ref[...]

    def proj(name, c0, c1):
        off, _ = _W_OFF[name]
        return jnp.dot(hb, w_ref[:, off + c0:off + c1], preferred_element_type=F32)

    def head_norm(y, w_row):
        n = y.shape[1]
        ss = jnp.dot((y * y).astype(BF16), bd[:n, :n], preferred_element_type=F32)
        return (y * lax.rsqrt(ss + EPS)) * jnp.concatenate([w_row] * (n // LANES), axis=1)

    def rope(y):
        sw = jnp.where(first_half, pltpu.roll(y, LANES - 16, 1), pltpu.roll(y, 16, 1))
        return y * cos + sw * sin

    deferred = []

    def normed(y, finish):
        waiting = list(deferred)
        deferred[:] = [(y, finish)]
        for py, pfinish in waiting:
            pfinish(py)

    def finish_qa(y, c0):
        y = head_norm(y, hw_ref[0:1, :])
        for j in range(0, y.shape[1], LANES):
            blk = rope(y[:, j:j + LANES]) * (HEAD_DIM ** -0.5 * LOG2E)
            qa_ref[0, :, c0 + j:c0 + j + LANES] = blk.astype(BF16)

    def finish_ka(y):
        ka_ref[0] = rope(head_norm(y, hw_ref[1:2, :])).astype(BF16)

    for c0 in range(0, A_Q, MXU_COLS):
        normed(proj("qa", c0, c0 + MXU_COLS), functools.partial(finish_qa, c0=c0))
    ka_dt = proj("ka", 0, A_KV + LANES)
    dt_ref[0] = ka_dt[:, A_KV:]
    normed(ka_dt[:, :A_KV], finish_ka)
    vt_ref[0] = lax.dot_general(wt_ref[...], hb, (((1,), (1,)), ((), ())),
                                preferred_element_type=F32).astype(BF16)
    for j in range(0, A_Q, 256):
        ga_ref[0, :, j:j + 256] = _silu(proj("ga", j, j + 256)).astype(BF16)
    def store_phases(ref, lanes, y, d):
        n = tm // d
        for r in range(d):
            ref[0, r, :, lanes] = y[r * n:(r + 1) * n, :]

    for g, (qr, kr, vr) in enumerate(((qb0_ref, kb0_ref, vb0_ref), (qb1_ref, kb1_ref, vb1_ref),
                                      (qb2_ref, kb2_ref, vb2_ref))):
        d = B_DILATIONS[g]
        if d == 1:
            lhs = hb
        else:
            lhs = jnp.concatenate(
                [jnp.concatenate([h_ref[j, pl.ds(r, tm // d, stride=d), :] for r in range(d)], axis=0)
                 for j in range(D_MODEL // LANES)], axis=1).astype(BF16)

        def proj_g(name, c0, c1, lhs=lhs):
            off, _ = _W_OFF[name]
            return jnp.dot(lhs, w_ref[:, off + c0:off + c1], preferred_element_type=F32)

        def finish_q(y, ref=qr, d=d):
            store_phases(ref, slice(0, B_OUT), (head_norm(y, hw_ref[2:3, :]) * (HEAD_DIM ** -0.5)).astype(BF16), d)

        def finish_k(y, ref=kr, d=d):
            store_phases(ref, slice(0, B_OUT), head_norm(y, hw_ref[3:4, :]).astype(BF16), d)

        normed(proj_g("qb", g * B_OUT, (g + 1) * B_OUT), finish_q)
        normed(proj_g("kb", g * B_OUT, (g + 1) * B_OUT), finish_k)
        v = proj_g("vb", g * B_OUT, (g + 1) * B_OUT)
        store_phases(vr, slice(0, B_OUT), v.astype(BF16), d)
    normed(None, None)
    deferred.clear()
    gb_ref[0] = _silu(proj("gb", 0, B_OUT)).astype(BF16)
    i = pl.program_id(1)
    n_t = pl.num_programs(1)
    halo = xp_ref.shape[1]
    ext_lhs = jnp.concatenate([modulated(xp_ref[0]).astype(BF16), hb, modulated(xn_ref[0]).astype(BF16)], axis=0)
    row = lax.broadcasted_iota(jnp.int32, (tm + 2 * halo, 1), 0)
    in_seq = jnp.logical_and(jnp.logical_or(row >= halo, i > 0),
                             jnp.logical_or(row < halo + tm, i < n_t - 1))
    off_xbc, _ = _W_OFF["xbc"]
    for j in range(0, C_XBC, 256):
        cols = slice(j, j + 256)
        ext = jnp.dot(ext_lhs, w_ref[:, off_xbc + j:off_xbc + j + 256], preferred_element_type=F32)
        ext_ref[:, cols] = jnp.where(in_seq, ext, 0.0)
        acc = jnp.zeros((tm, 256), F32) + cb_ref[:, cols]
        for k in range(C_CONV):
            start = halo + k - C_CONV // 2
            acc = acc + ext_ref[start:start + tm, cols] * cw_ref[k:k + 1, cols]
        xbc_ref[0, :, cols] = _silu(acc).astype(BF16)
    for j in range(0, C_INNER, 256):
        zc_ref[0, :, j:j + 256] = _silu(proj("zc", j, j + 256)).astype(BF16)
    for j in range(0, N_BRANCH * D_MODEL, 512):
        mg_ref[0, :, j:j + 512] = jax.nn.sigmoid(proj("mg", j, j + 512) + bg_ref[:, j:j + 512]).astype(BF16)


def _in_projection(x, sc1p, shift, norm_w, cos_t, sin_t, head_w, b_gate, conv_w, conv_b, w_main, w_t, tm=256):
    bsz, s, d = x.shape
    nt = s // tm
    hb = tm // CONV_HALO
    n_h = s // CONV_HALO
    row = lambda i, w: pl.BlockSpec((1, tm, w), lambda b, i: (b, i, 0))
    const2 = lambda shp: pl.BlockSpec(shp, lambda b, i: (0, 0))
    phase = [("phase", dil) for _ in range(3) for dil in B_DILATIONS]
    outs = ([("qa", A_Q, BF16), ("ka", A_KV, BF16), None, ("ga", A_Q, BF16)] + phase +
            [("gb", B_OUT, BF16), ("xbc", C_XBC, BF16),
             ("zc", C_INNER, BF16), ("dt", LANES, F32), ("mg", N_BRANCH * D_MODEL, BF16)])
    out_shape, out_specs = [], []
    for o in outs:
        if o is None:
            out_shape.append(jax.ShapeDtypeStruct((bsz, A_KV, s), BF16))
            out_specs.append(pl.BlockSpec((1, A_KV, tm), lambda b, i: (b, 0, i)))
        elif o[0] == "phase":
            out_shape.append(jax.ShapeDtypeStruct((bsz, o[1], s // o[1], B_OUT), BF16))
            out_specs.append(pl.BlockSpec((1, o[1], tm // o[1], B_OUT), lambda b, i: (b, 0, i, 0)))
        else:
            out_shape.append(jax.ShapeDtypeStruct((bsz, s, o[1]), o[2]))
            out_specs.append(row(0, o[1]))
    return pl.pallas_call(
        _inproj_kernel,
        grid=(bsz, nt),
        in_specs=[row(0, d),
                  pl.BlockSpec((1, CONV_HALO, d), lambda b, i: (b, jnp.maximum(i * hb - 1, 0), 0)),
                  pl.BlockSpec((1, CONV_HALO, d), lambda b, i: (b, jnp.minimum((i + 1) * hb, n_h - 1), 0)),
                  pl.BlockSpec((1, 1, d), lambda b, i: (b, 0, 0)),
                  pl.BlockSpec((1, 1, d), lambda b, i: (b, 0, 0)),
                  const2((1, d)),
                  pl.BlockSpec((tm, LANES), lambda b, i: (i, 0)),
                  pl.BlockSpec((tm, LANES), lambda b, i: (i, 0)),
                  const2((8, LANES)),
                  const2(b_gate.shape),
                  const2(conv_w.shape),
                  const2(conv_b.shape),
                  const2(w_main.shape),
                  const2(w_t.shape)],
        out_specs=out_specs,
        out_shape=out_shape,
        scratch_shapes=[pltpu.VMEM((d // LANES, tm, LANES), F32),
                        pltpu.VMEM((tm + 2 * CONV_HALO, C_XBC), F32)],
        compiler_params=_cparams(("arbitrary", "arbitrary")),
        name="norm_inproj",
    )(x, x, x, sc1p, shift, norm_w, cos_t, sin_t, head_w, b_gate, conv_w, conv_b, w_main, w_t)


A_AUG = HEAD_DIM + 8


def _attn_a_kernel(bounded_ref, q_ref, k_ref, vt_ref, o_ref, qt_ref, m_ref, acc_ref, st_ref, *, kc):
    tq = q_ref.shape[1]
    s_len = k_ref.shape[1]
    n_blk = A_Q // LANES
    lane = lax.broadcasted_iota(jnp.int32, (tq, LANES), 1)
    for j in range(n_blk):
        qblk = q_ref[0, :, j * LANES:(j + 1) * LANES]
        for kv in range(A_KV_HEADS):
            in_half = (lane < HEAD_DIM) if kv == 0 else (lane >= HEAD_DIM)
            qt_ref[A_KV_HEADS * j + kv] = jnp.where(in_half, qblk, jnp.zeros_like(qblk))
    m_ref[...] = jnp.full(m_ref.shape, NEG_INF, F32)
    acc_ref[...] = jnp.zeros(acc_ref.shape, F32)
    ones = jnp.ones((A_AUG - HEAD_DIM, kc), BF16)

    n_chunks = s_len // kc

    def scores(c, h):
        off = pl.multiple_of(c * kc, kc)
        return lax.dot_general(k_ref[0, pl.ds(off, kc), :], qt_ref[h], (((1,), (1,)), ((), ())),
                               preferred_element_type=F32)

    ahead = st_ref.shape[0]

    def sweep(online, unroll):
        for i in range(ahead):
            st_ref[i] = scores(0, i)

        def vaug(c, kv):
            off = pl.multiple_of(c * kc, kc)
            return jnp.concatenate([vt_ref[0, kv * HEAD_DIM:(kv + 1) * HEAD_DIM, pl.ds(off, kc)], ones], axis=0)

        def body(cc, carry):
            pending = [st_ref[i] for i in range(ahead)]
            for t in range(unroll * A_HEADS):
                c, h = cc * unroll + t // A_HEADS, t % A_HEADS
                nxt = t + ahead
                c_nxt = jnp.minimum(cc * unroll + nxt // A_HEADS, n_chunks - 1)
                pending.append(scores(c_nxt, nxt % A_HEADS))
                st = pending.pop(0)
                v = vaug(c, h % A_KV_HEADS)
                if online:
                    m = m_ref[h:h + 1, :]
                    m_new = jnp.maximum(m, jnp.max(st, axis=0, keepdims=True))
                    alpha = jnp.exp2(m - m_new)
                    p = jnp.exp2(st - m_new).astype(BF16)
                    acc_ref[h] = alpha * acc_ref[h] + jnp.dot(v, p, preferred_element_type=F32)
                    m_ref[h:h + 1, :] = m_new
                else:
                    p = jnp.exp2(st).astype(BF16)
                    acc_ref[h] += jnp.dot(v, p, preferred_element_type=F32)
            for i in range(ahead):
                st_ref[i] = pending[i]
            return carry

        lax.fori_loop(0, n_chunks // unroll, body, 0)

    bounded = bounded_ref[0] == 1
    pl.when(bounded)(lambda: sweep(False, 8))
    pl.when(jnp.logical_not(bounded))(lambda: sweep(True, 1))

    for j in range(n_blk):
        halves = []
        for kv in range(A_KV_HEADS):
            a = acc_ref[A_KV_HEADS * j + kv]
            halves.append(a[:HEAD_DIM] / a[HEAD_DIM:HEAD_DIM + 1])
        o_ref[0, :, j * LANES:(j + 1) * LANES] = jnp.concatenate(halves, axis=0).T.astype(o_ref.dtype)


A_SCORE_LIMIT = 40.0


def _score_bound(q_norm, k_norm):
    return (1.05 * HEAD_DIM * HEAD_DIM ** -0.5 * LOG2E) * jnp.max(jnp.abs(q_norm)) * jnp.max(jnp.abs(k_norm))


def _mixer_a(qa, ka, vt, bound, tq=256, kc=512):
    bsz, s, _ = qa.shape
    bounded = (bound <= A_SCORE_LIMIT).astype(jnp.int32).reshape(1)
    return pl.pallas_call(
        functools.partial(_attn_a_kernel, kc=kc),
        grid=(bsz, s // tq),
        in_specs=[pl.BlockSpec(memory_space=pltpu.SMEM),
                  pl.BlockSpec((1, tq, A_Q), lambda b, i: (b, i, 0)),
                  pl.BlockSpec((1, s, A_KV), lambda b, i: (b, 0, 0)),
                  pl.BlockSpec((1, A_KV, s), lambda b, i: (b, 0, 0))],
        out_specs=pl.BlockSpec((1, tq, A_Q), lambda b, i: (b, i, 0)),
        out_shape=jax.ShapeDtypeStruct((bsz, s, A_Q), BF16),
        scratch_shapes=[pltpu.VMEM((A_HEADS, tq, LANES), BF16),
                        pltpu.VMEM((A_HEADS, tq), F32),
                        pltpu.VMEM((A_HEADS, A_AUG, tq), F32),
                        pltpu.VMEM((3, kc, tq), F32)],
        compiler_params=_cparams(("arbitrary", "arbitrary")),
        name="mixer_a_attn",
    )(bounded, qa, ka, vt)


B_SIDE = 64
B_TQ = 256
B_BAND = B_TQ + 2 * B_SIDE
B_STEP = 2 * B_TQ


def _attn_b_kernel(q_ref, kp_ref, kc_ref, kn_ref, vp_ref, vc_ref, vn_ref, bias_ref, o_ref, lse_ref):
    mi = pl.program_id(2)
    n_m = pl.num_programs(2)
    n_sub = q_ref.shape[2] // B_TQ
    kall = jnp.concatenate([kp_ref[0, 0], kc_ref[0, 0], kn_ref[0, 0]], axis=0)
    vall = jnp.concatenate([vp_ref[0, 0], vc_ref[0, 0], vn_ref[0, 0]], axis=0)
    kk = lax.broadcasted_iota(jnp.int32, (1, B_BAND), 1)
    lane = lax.broadcasted_iota(jnp.int32, (B_TQ, B_OUT), 1) // HEAD_DIM
    for t in range(n_sub):
        q = q_ref[0, 0, t * B_TQ:(t + 1) * B_TQ, :]
        kband = kall[t * B_TQ:t * B_TQ + B_BAND]
        vband = vall[t * B_TQ:t * B_TQ + B_BAND]
        key_ok = None
        if t == 0:
            key_ok = jnp.logical_or(kk >= B_SIDE, mi > 0)
        if t == n_sub - 1:
            hi_ok = jnp.logical_or(kk < B_SIDE + B_TQ, mi < n_m - 1)
            key_ok = hi_ok if key_ok is None else jnp.logical_and(key_ok, hi_ok)
        o_acc = jnp.zeros((B_TQ, B_OUT), F32)
        lse_acc = jnp.zeros((B_TQ, B_OUT), F32)
        scores = [lax.dot_general(jnp.where(lane == h, q, jnp.zeros_like(q)), kband, (((1,), (1,)), ((), ())),
                                  preferred_element_type=F32) for h in range(B_HEADS_PER_GROUP)]
        for h in range(B_HEADS_PER_GROUP):
            s = scores[h] + bias_ref[h]
            if key_ok is not None:
                s = jnp.where(key_ok, s, NEG_INF)
            mx = jnp.max(s, axis=-1, keepdims=True)
            e = jnp.exp(s - mx)
            den = jnp.sum(e, axis=-1, keepdims=True)
            p = (e / den).astype(BF16)
            oh = jnp.dot(p, vband, preferred_element_type=F32)
            o_acc = jnp.where(lane == h, oh, o_acc)
            lse_acc = jnp.where(lane == h, mx + jnp.log(den), lse_acc)
        o_ref[0, 0, t * B_TQ:(t + 1) * B_TQ, :] = o_acc.astype(o_ref.dtype)
        lse_ref[0, 0, t * B_TQ:(t + 1) * B_TQ, :] = lse_acc


def _mixer_b_group(qb, kb, vb, bias_g, g):
    bsz, d, m, _ = qb.shape
    assert m % B_STEP == 0, (m, B_STEP)
    n_m = m // B_STEP
    hb = B_STEP // B_SIDE
    n_side = m // B_SIDE
    cur = pl.BlockSpec((1, 1, B_STEP, B_OUT), lambda b, r, i: (b, r, i, 0))
    prev = pl.BlockSpec((1, 1, B_SIDE, B_OUT), lambda b, r, i: (b, r, jnp.maximum(i * hb - 1, 0), 0))
    nxt = pl.BlockSpec((1, 1, B_SIDE, B_OUT),
                       lambda b, r, i: (b, r, jnp.minimum((i + 1) * hb, n_side - 1), 0))
    return pl.pallas_call(
        _attn_b_kernel,
        grid=(bsz, d, n_m),
        in_specs=[cur, prev, cur, nxt, prev, cur, nxt,
                  pl.BlockSpec((B_HEADS_PER_GROUP, B_TQ, B_BAND), lambda b, r, i: (0, 0, 0))],
        out_specs=[cur, cur],
        out_shape=[jax.ShapeDtypeStruct((bsz, d, m, B_OUT), BF16),
                   jax.ShapeDtypeStruct((bsz, d, m, B_OUT), F32)],
        compiler_params=_cparams(("arbitrary", "arbitrary", "arbitrary")),
        name=f"mixer_b_attn_g{g}",
    )(qb, kb, kb, kb, vb, vb, vb, bias_g)


def _t5_bucket(rel):
    nb = N_BUCKETS // 2
    max_exact = nb // 2
    ret = jnp.where(rel > 0, nb, 0)
    n = jnp.abs(rel)
    nf = jnp.maximum(n, 1).astype(F32)
    large = max_exact + (jnp.log(nf / max_exact) / math.log(MAX_DISTANCE / max_exact)
                         * (nb - max_exact)).astype(jnp.int32)
    large = jnp.minimum(large, nb - 1)
    return ret + jnp.where(n < max_exact, n, large)


def _band_bias(rel_bias):
    rel = jnp.arange(-B_SIDE, B_SIDE + 1)
    n_rel = 2 * B_SIDE + 1
    width = B_TQ + B_BAND
    nh = B_HEADS_PER_GROUP
    tables = []
    for g in range(B_GROUPS):
        tbl = rel_bias[:, g * nh:(g + 1) * nh]
        b1 = tbl[_t5_bucket(rel * B_DILATIONS[g])].T.astype(F32)
        w = jnp.concatenate([b1, jnp.full((nh, width - n_rel), NEG_INF, F32)], axis=1)
        flat = jnp.broadcast_to(w[:, None, :], (nh, B_TQ, width)).reshape(nh, B_TQ * width)
        tables.append(flat[:, :B_TQ * (width - 1)].reshape(nh, B_TQ, width - 1)[:, :, :B_BAND])
    return jnp.stack(tables, axis=0)


CONV_HALO = 8


def _split3(x):
    hi = x.astype(BF16)
    r1 = x - hi.astype(F32)
    mid = r1.astype(BF16)
    lo = (r1 - mid.astype(F32)).astype(BF16)
    return hi, mid, lo


def _ssd_direction(x_ref, dt_ref, prm_ref, state_ref, y_ref, *, reverse):
    L = C_CHUNK
    col0 = C_HEADS if reverse else 0
    r = lax.broadcasted_iota(jnp.int32, (L, L), 0)
    c = lax.broadcasted_iota(jnp.int32, (L, L), 1)
    causal = (r <= c) if reverse else (r >= c)
    tri = jnp.where(causal, 1.0, 0.0).astype(BF16)

    groups = []
    for g in range(C_GROUPS):
        lanes = slice(g * 256, (g + 1) * 256)
        xg = x_ref[0, :, lanes].astype(F32)
        bm = x_ref[0, :, C_INNER + g * C_STATE:C_INNER + (g + 1) * C_STATE]
        cm = x_ref[0, :, C_INNER + C_BC + g * C_STATE:C_INNER + C_BC + (g + 1) * C_STATE]
        cb = lax.dot_general(cm, bm, (((1,), (1,)), ((), ())), preferred_element_type=F32)
        state = state_ref[g]
        y_in = jnp.dot(cm, state.astype(BF16), preferred_element_type=F32)
        groups.append((lanes, xg, bm, cb, state, y_in))
    yield

    dt_bias = prm_ref[0:1, :]
    a_neg = -jnp.exp(prm_ref[1:2, :])
    dtv = jax.nn.softplus(dt_ref[0] + dt_bias)
    a = dtv * a_neg
    hi, mid, lo = _split3(a)
    cs2 = jnp.dot(tri, jnp.concatenate([hi, mid], axis=1), preferred_element_type=F32)
    cs = cs2[:, :LANES] + cs2[:, LANES:] + jnp.dot(tri, lo, preferred_element_type=F32)
    yield
    cs_t = cs.T
    tot = cs[0:1, :] if reverse else cs[L - 1:L, :]
    decay_in = jnp.exp(cs)
    decay_out = jnp.exp(tot - cs)
    chunk_decay = jnp.exp(tot)

    lane_head = lax.broadcasted_iota(jnp.int32, (1, 4 * C_HEAD_DIM), 1) // C_HEAD_DIM

    def expand_row(m, g):
        out = jnp.zeros((1, 4 * C_HEAD_DIM), F32)
        for j in range(4):
            cj = col0 + g * 4 + j
            out = jnp.where(lane_head == j, m[:, cj:cj + 1], out)
        return out

    sel = (lax.broadcasted_iota(jnp.int32, (LANES, C_INNER), 0)
           == col0 + lax.broadcasted_iota(jnp.int32, (LANES, C_INNER), 1) // C_HEAD_DIM)
    sel = jnp.where(sel, 1.0, 0.0).astype(BF16)

    sel2 = jnp.concatenate([sel, sel], axis=0)

    def expand(m):
        hi = m.astype(BF16)
        lo = (m - hi.astype(F32)).astype(BF16)
        return jnp.dot(jnp.concatenate([hi, lo], axis=1), sel2, preferred_element_type=F32)

    dt_e, din_e, dout_e = expand(dtv), expand(decay_in), expand(decay_out)
    yield

    for g, (lanes, xg, bm, cb, state, y_in) in enumerate(groups):
        xdt = xg * dt_e[:, lanes]
        y = y_in * din_e[:, lanes]
        xdt_b = xdt.astype(BF16)
        parts = []
        for j in range(4):
            cj = col0 + g * 4 + j
            diff = cs[:, cj:cj + 1] - cs_t[cj:cj + 1, :]
            lm = jnp.exp(jnp.where(causal, diff, NEG_INF))
            gmat = (cb * lm).astype(BF16)
            parts.append(jnp.dot(gmat, xdt_b[:, j * C_HEAD_DIM:(j + 1) * C_HEAD_DIM],
                                 preferred_element_type=F32))
        y = y + jnp.concatenate(parts, axis=1)
        if not reverse:
            y = y + xg * expand_row(prm_ref[2:3, :], g)
        y_ref[0, :, lanes] = y.astype(y_ref.dtype)
        xs = (xdt * dout_e[:, lanes]).astype(BF16)
        upd = lax.dot_general(bm, xs, (((0,), (0,)), ((), ())), preferred_element_type=F32)
        state_ref[g] = state * expand_row(chunk_decay, g) + upd
    yield


def _ssd_kernel(xf_ref, xb_ref, dtf_ref, dtb_ref, prm_ref, yf_ref, yb_ref, sf_ref, sb_ref):
    @pl.when(pl.program_id(1) == 0)
    def _():
        sf_ref[...] = jnp.zeros_like(sf_ref)
        sb_ref[...] = jnp.zeros_like(sb_ref)

    forward = _ssd_direction(xf_ref, dtf_ref, prm_ref, sf_ref, yf_ref, reverse=False)
    backward = _ssd_direction(xb_ref, dtb_ref, prm_ref, sb_ref, yb_ref, reverse=True)
    for _ in zip(forward, backward):
        pass


def _ssd(xact, dt, prm):
    bsz, s, ch = xact.shape
    n_c = s // C_CHUNK
    fwd = lambda w: pl.BlockSpec((1, C_CHUNK, w), lambda b, i: (b, i, 0))
    bwd = lambda w: pl.BlockSpec((1, C_CHUNK, w), lambda b, i: (b, n_c - 1 - i, 0))
    return pl.pallas_call(
        _ssd_kernel,
        grid=(bsz, n_c),
        in_specs=[fwd(ch), bwd(ch), fwd(LANES), bwd(LANES), pl.BlockSpec((8, LANES), lambda b, i: (0, 0))],
        out_specs=[fwd(C_INNER), bwd(C_INNER)],
        out_shape=[jax.ShapeDtypeStruct((bsz, s, C_INNER), BF16)] * 2,
        scratch_shapes=[pltpu.VMEM((C_GROUPS, C_STATE, 4 * C_HEAD_DIM), F32)] * 2,
        compiler_params=_cparams(("arbitrary", "arbitrary")),
        name="mixer_c_ssd",
    )(xact, xact, dt, dt, prm)


def _merge_kernel(x_ref, gate_ref, oa_ref, ga_ref, ob0_ref, ob1_ref, ob2_ref, ls0_ref, ls1_ref, ls2_ref,
                  gb_ref, yf_ref, yb_ref, zc_ref, mg_ref, nw_ref, wa_ref, wb_ref, wc_ref, wo_ref,
                  o_ref, il_ref):
    ya = (oa_ref[0].astype(F32) * ga_ref[0].astype(F32)).astype(BF16)

    def token_order(ref, slot):
        d, n = ref.shape[1], ref.shape[2]
        if d == 1:
            return ref[0, 0].astype(F32)
        halves = B_OUT // LANES
        for r in range(d):
            for j in range(halves):
                il_ref[slot * halves + j, pl.ds(r, n, stride=d), :] = (
                    ref[0, r, :, j * LANES:(j + 1) * LANES].astype(F32))
        return jnp.concatenate([il_ref[slot * halves + j] for j in range(halves)], axis=1)

    l0, l1, l2 = token_order(ls0_ref, 0), token_order(ls1_ref, 0), token_order(ls2_ref, 1)
    o0, o1, o2 = token_order(ob0_ref, 0), token_order(ob1_ref, 2), token_order(ob2_ref, 3)
    mx = jnp.maximum(jnp.maximum(l0, l1), l2)
    e0, e1, e2 = jnp.exp(l0 - mx), jnp.exp(l1 - mx), jnp.exp(l2 - mx)
    yb = (e0 * o0 + e1 * o1 + e2 * o2) / (e0 + e1 + e2)
    yb = (yb * gb_ref[0].astype(F32)).astype(BF16)
    y = (yf_ref[0].astype(F32) + yb_ref[0].astype(F32)) * zc_ref[0].astype(F32)
    yc = ((y * lax.rsqrt(jnp.mean(y * y, axis=-1, keepdims=True) + EPS)) * nw_ref[...]).astype(BF16)

    d = D_MODEL
    gates = lambda k: mg_ref[0, :, k * d:(k + 1) * d].astype(F32)
    merged = gates(0) * jnp.dot(ya, wa_ref[...], preferred_element_type=F32)
    merged = merged + gates(1) * jnp.dot(yb, wb_ref[...], preferred_element_type=F32)
    merged = merged + gates(2) * jnp.dot(yc, wc_ref[...], preferred_element_type=F32)
    out = jnp.dot(merged.astype(BF16), wo_ref[...], preferred_element_type=F32)
    o_ref[0] = x_ref[0] + gate_ref[0] * out


def _merge(x, gate, oa, ga, ob, lse, gb, yf, yb, zc, mg, ssm_w, wa, wb, wc, wo, tm=256):
    bsz, s, d = x.shape
    row = lambda w: pl.BlockSpec((1, tm, w), lambda b, i: (b, i, 0))
    phase = [pl.BlockSpec((1, dil, tm // dil, B_OUT), lambda b, i: (b, 0, i, 0)) for dil in B_DILATIONS]
    const2 = lambda shp: pl.BlockSpec(shp, lambda b, i: (0, 0))
    return pl.pallas_call(
        _merge_kernel,
        grid=(bsz, s // tm),
        in_specs=[row(d), pl.BlockSpec((1, 1, d), lambda b, i: (b, 0, 0)),
                  row(A_Q), row(A_Q)] + phase + phase +
                 [row(B_OUT), row(C_INNER), row(C_INNER), row(C_INNER),
                  row(N_BRANCH * d), const2((1, C_INNER)),
                  const2(wa.shape), const2(wb.shape), const2(wc.shape), const2(wo.shape)],
        out_specs=row(d),
        out_shape=jax.ShapeDtypeStruct((bsz, s, d), F32),
        scratch_shapes=[pltpu.VMEM((4 * B_OUT // LANES, tm, LANES), F32)],
        compiler_params=_cparams(("arbitrary", "arbitrary")),
        name="merge_out",
    )(x, gate, oa, ga, *ob, *lse, gb, yf, yb, zc, mg, ssm_w, wa, wb, wc, wo)


def _rope_tables(s):
    quarter = HEAD_DIM // 4
    freqs = ROPE_THETA ** (-jnp.arange(quarter, dtype=F32) / quarter)
    t = jnp.arange(s)
    ang_r = (t // GRID_W).astype(F32)[:, None] * freqs
    ang_c = (t % GRID_W).astype(F32)[:, None] * freqs
    cos = jnp.concatenate([jnp.cos(ang_r)] * 2 + [jnp.cos(ang_c)] * 2, axis=-1)
    sin = jnp.concatenate([-jnp.sin(ang_r), jnp.sin(ang_r), -jnp.sin(ang_c), jnp.sin(ang_c)], axis=-1)
    return jnp.tile(cos, (1, LANES // HEAD_DIM)), jnp.tile(sin, (1, LANES // HEAD_DIM))


def _layer_weights(w_in_l):
    cols = lambda k: w_in_l[:, SPLIT_OFFSETS[k]:SPLIT_OFFSETS[k + 1]]
    d = w_in_l.shape[0]
    perm = lambda w: w.reshape(d, A_HEADS, HEAD_DIM)[:, A_HEAD_ORDER, :].reshape(d, A_Q)
    dt_cols = jnp.concatenate([cols(12), cols(13), jnp.zeros((d, LANES - 2 * C_HEADS), w_in_l.dtype)], axis=1)
    pieces = {"qa": perm(cols(0)), "ka": cols(1), "ga": perm(cols(3)), "qb": cols(4), "kb": cols(5),
              "vb": cols(6), "gb": cols(7), "xbc": jnp.concatenate([cols(8), cols(10), cols(11)], axis=1),
              "zc": cols(9), "dt": dt_cols, "mg": cols(14)}
    w_main = jnp.concatenate([pieces[n] for n, _ in _W_GROUPS], axis=1).astype(BF16)
    w_t = cols(2).T.astype(BF16)
    return w_main, w_t


def _pad_row(v, width=LANES):
    return jnp.pad(v.astype(F32), (0, width - v.shape[0]))


def kernel(x, c, norm_w, w_ada, b_ada, w_in, b_gate, q_norm_a, k_norm_a, q_norm_b, k_norm_b, rel_bias,
           conv_w, conv_b, a_log, dt_bias, d_skip, ssm_norm_w, w_proj_a, w_proj_b, w_proj_c, w_out):
    bsz, s, d = x.shape
    depth = w_in.shape[0]
    c_pad = jnp.pad(c, ((0, 8 - bsz), (0, 0)))
    mod = _modulation(c_pad, w_ada, b_ada)[:, :bsz]
    cos_t, sin_t = _rope_tables(s)
    bias_tbl = _band_bias(rel_bias)
    tile2 = lambda v: jnp.tile(v.astype(F32), LANES // HEAD_DIM)
    for l in range(depth):
        shift, scale, gate = (mod[l, :, k * d:(k + 1) * d].reshape(bsz, 1, d) for k in range(3))
        w_main, w_t = _layer_weights(w_in[l])
        head_w = jnp.stack([tile2(q_norm_a[l]), tile2(k_norm_a[l]), tile2(q_norm_b[l]), tile2(k_norm_b[l])]
                           + [jnp.zeros((LANES,), F32)] * 4, axis=0)
        (qa, ka, vt, ga, *qkv_b, gb, xbc, zc, dt, mg) = _in_projection(
            x, 1.0 + scale, shift, norm_w[l].reshape(1, d), cos_t, sin_t, head_w,
            b_gate[l].reshape(1, -1), jnp.pad(conv_w[l], ((0, 8 - C_CONV), (0, 0))),
            conv_b[l].reshape(1, C_XBC), w_main, w_t)
        oa = _mixer_a(qa, ka, vt, _score_bound(q_norm_a[l], k_norm_a[l]))
        ob, lse = zip(*[_mixer_b_group(qkv_b[g], qkv_b[B_GROUPS + g], qkv_b[2 * B_GROUPS + g], bias_tbl[g], g)
                        for g in range(B_GROUPS)])
        prm = jnp.stack([_pad_row(dt_bias[l].reshape(-1)), _pad_row(a_log[l].reshape(-1)),
                         _pad_row(jnp.tile(d_skip[l], 2))] + [jnp.zeros((LANES,), F32)] * 5, axis=0)
        yf, yb = _ssd(xbc, dt, prm)
        perm_rows = lambda w: w.reshape(A_HEADS, HEAD_DIM, d)[A_HEAD_ORDER, :, :].reshape(A_Q, d)
        x = _merge(x, gate, oa, ga, ob, lse, gb, yf, yb, zc, mg, ssm_norm_w[l].reshape(1, -1),
                   perm_rows(w_proj_a[l]).astype(BF16), w_proj_b[l].astype(BF16),
                   w_proj_c[l].astype(BF16), w_out[l].astype(BF16))
    return x
```

```python
import functools
import math

import numpy as np
import jax
import jax.numpy as jnp
from jax import lax
from jax.experimental import pallas as pl
from jax.experimental.pallas import tpu as pltpu

F32 = jnp.float32
BF16 = jnp.bfloat16

D_MODEL = 1024
HEAD_DIM = 64
EPS = 1e-6
NEG_INF = -1e30
GRID_W = 64
A_HEADS = 8
A_KV_HEADS = 2
ROPE_THETA = 10000.0
B_GROUPS = 3
B_HEADS_PER_GROUP = 4
B_WINDOWS = (128, 512, 2048)
B_DILATIONS = (1, 4, 16)
N_BUCKETS = 32
MAX_DISTANCE = 1024
C_HEADS = 8
C_HEAD_DIM = 64
C_INNER = C_HEADS * C_HEAD_DIM
C_GROUPS = 2
C_STATE = 128
C_CONV = 5
C_CHUNK = 128
N_BRANCH = 3

A_Q = A_HEADS * HEAD_DIM
A_KV = A_KV_HEADS * HEAD_DIM
B_QKV = B_GROUPS * B_HEADS_PER_GROUP * HEAD_DIM
B_OUT = B_HEADS_PER_GROUP * HEAD_DIM
C_BC = C_GROUPS * C_STATE
C_XBC = C_INNER + 2 * C_BC
SPLIT_WIDTHS = (A_Q, A_KV, A_KV, A_Q, B_QKV, B_QKV, B_QKV, B_OUT,
                C_INNER, C_INNER, C_BC, C_BC, C_HEADS, C_HEADS, N_BRANCH * D_MODEL)
SPLIT_OFFSETS = tuple(int(v) for v in np.cumsum((0,) + SPLIT_WIDTHS))

LANES = 128
VMEM_LIMIT = 56 * 1024 * 1024
LOG2E = math.log2(math.e)

_W_GROUPS = (("qa", A_Q), ("ka", A_KV), ("dt", LANES), ("ga", A_Q), ("qb", B_QKV), ("kb", B_QKV), ("vb", B_QKV),
             ("gb", B_OUT), ("xbc", C_XBC), ("zc", C_INNER), ("mg", N_BRANCH * D_MODEL))
MXU_COLS = 256
_W_OFF = {}
_o = 0
for _n, _w in _W_GROUPS:
    _W_OFF[_n] = (_o, _w)
    _o += _w
W_MAIN_COLS = _o

A_HEAD_ORDER = tuple(h for j in range(4) for h in (j, j + 4))


def _cparams(sem):
    return pltpu.CompilerParams(dimension_semantics=sem, vmem_limit_bytes=VMEM_LIMIT)


def _silu(x):
    return x * jax.nn.sigmoid(x)


def _mod_kernel(c_ref, w_ref, b_ref, o_ref):
    c = c_ref[...]
    o_ref[0] = jnp.dot(_silu(c), w_ref[0], preferred_element_type=F32,
                       precision=lax.Precision.HIGHEST) + b_ref[0]


def _modulation(c_pad, w_ada, b_ada):
    depth, d, n = w_ada.shape
    tn = 512
    return pl.pallas_call(
        _mod_kernel,
        grid=(depth, n // tn),
        in_specs=[pl.BlockSpec((c_pad.shape[0], d), lambda l, j: (0, 0)),
                  pl.BlockSpec((1, d, tn), lambda l, j: (l, 0, j)),
                  pl.BlockSpec((1, 1, tn), lambda l, j: (l, 0, j))],
        out_specs=pl.BlockSpec((1, c_pad.shape[0], tn), lambda l, j: (l, 0, j)),
        out_shape=jax.ShapeDtypeStruct((depth, c_pad.shape[0], n), F32),
        compiler_params=_cparams(("arbitrary", "arbitrary")),
        name="adaln_mod",
    )(c_pad, w_ada, b_ada.reshape(depth, 1, n))


def _head_sumsq_matrix():
    r = lax.broadcasted_iota(jnp.int32, (MXU_COLS, MXU_COLS), 0) // HEAD_DIM
    c = lax.broadcasted_iota(jnp.int32, (MXU_COLS, MXU_COLS), 1) // HEAD_DIM
    return jnp.where(r == c, 1.0 / HEAD_DIM, 0.0).astype(BF16)


def _inproj_kernel(x_ref, xp_ref, xn_ref, sc_ref, sh_ref, nw_ref, cos_ref, sin_ref, hw_ref, bg_ref,
                   cw_ref, cb_ref, w_ref, wt_ref,
                   qa_ref, ka_ref, vt_ref, ga_ref,
                   qb0_ref, qb1_ref, qb2_ref, kb0_ref, kb1_ref, kb2_ref, vb0_ref, vb1_ref, vb2_ref,
                   gb_ref, xbc_ref, zc_ref, dt_ref, mg_ref, h_ref, ext_ref):
    tm = x_ref.shape[1]

    def modulated(xv):
        ms = jnp.mean(xv * xv, axis=-1, keepdims=True)
        return ((xv * lax.rsqrt(ms + EPS)) * nw_ref[...]) * sc_ref[0] + sh_ref[0]

    h = modulated(x_ref[0])
    hb = h.astype(BF16)
    for j in range(D_MODEL // LANES):
        h_ref[j] = h[:, j * LANES:(j + 1) * LANES]

    bd = _head_sumsq_matrix()
    lane = lax.broadcasted_iota(jnp.int32, (1, LANES), 1)
    first_half = (lane % 32) < 16
    cos = cos_ref[...]
    sin = sin_ref[...]

    def proj(name, c0, c1):
        off, _ = _W_OFF[name]
        return jnp.dot(hb, w_ref[:, off + c0:off + c1], preferred_element_type=F32)

    def head_norm(y, w_row):
        n = y.shape[1]
        ss = jnp.dot((y * y).astype(BF16), bd[:n, :n], preferred_element_type=F32)
        return (y * lax.rsqrt(ss + EPS)) * jnp.concatenate([w_row] * (n // LANES), axis=1)

    def rope(y):
        sw = jnp.where(first_half, pltpu.roll(y, LANES - 16, 1), pltpu.roll(y, 16, 1))
        return y * cos + sw * sin

    deferred = []

    def normed(y, finish):
        waiting = list(deferred)
        deferred[:] = [(y, finish)]
        for py, pfinish in waiting:
            pfinish(py)

    def finish_qa(y, c0):
        y = head_norm(y, hw_ref[0:1, :])
        for j in range(0, y.shape[1], LANES):
            blk = rope(y[:, j:j + LANES]) * (HEAD_DIM ** -0.5 * LOG2E)
            qa_ref[0, :, c0 + j:c0 + j + LANES] = blk.astype(BF16)

    def finish_ka(y):
        ka_ref[0] = rope(head_norm(y, hw_ref[1:2, :])).astype(BF16)

    for c0 in range(0, A_Q, MXU_COLS):
        normed(proj("qa", c0, c0 + MXU_COLS), functools.partial(finish_qa, c0=c0))
    ka_dt = proj("ka", 0, A_KV + LANES)
    dt_ref[0] = ka_dt[:, A_KV:]
    normed(ka_dt[:, :A_KV], finish_ka)
    vt_ref[0] = lax.dot_general(wt_ref[...], hb, (((1,), (1,)), ((), ())),
                                preferred_element_type=F32).astype(BF16)
    for j in range(0, A_Q, 256):
        ga_ref[0, :, j:j + 256] = _silu(proj("ga", j, j + 256)).astype(BF16)
    def store_phases(ref, lanes, y, d):
        n = tm // d
        for r in range(d):
            ref[0, r, :, lanes] = y[r * n:(r + 1) * n, :]

    for g, (qr, kr, vr) in enumerate(((qb0_ref, kb0_ref, vb0_ref), (qb1_ref, kb1_ref, vb1_ref),
                                      (qb2_ref, kb2_ref, vb2_ref))):
        d = B_DILATIONS[g]
        if d == 1:
            lhs = hb
        else:
            lhs = jnp.concatenate(
                [jnp.concatenate([h_ref[j, pl.ds(r, tm // d, stride=d), :] for r in range(d)], axis=0)
                 for j in range(D_MODEL // LANES)], axis=1).astype(BF16)

        def proj_g(name, c0, c1, lhs=lhs):
            off, _ = _W_OFF[name]
            return jnp.dot(lhs, w_ref[:, off + c0:off + c1], preferred_element_type=F32)

        def finish_q(y, ref=qr, d=d):
            store_phases(ref, slice(0, B_OUT), (head_norm(y, hw_ref[2:3, :]) * (HEAD_DIM ** -0.5)).astype(BF16), d)

        def finish_k(y, ref=kr, d=d):
            store_phases(ref, slice(0, B_OUT), head_norm(y, hw_ref[3:4, :]).astype(BF16), d)

        normed(proj_g("qb", g * B_OUT, (g + 1) * B_OUT), finish_q)
        normed(proj_g("kb", g * B_OUT, (g + 1) * B_OUT), finish_k)
        v = proj_g("vb", g * B_OUT, (g + 1) * B_OUT)
        store_phases(vr, slice(0, B_OUT), v.astype(BF16), d)
    normed(None, None)
    deferred.clear()
    gb_ref[0] = _silu(proj("gb", 0, B_OUT)).astype(BF16)
    i = pl.program_id(1)
    n_t = pl.num_programs(1)
    halo = xp_ref.shape[1]
    ext_lhs = jnp.concatenate([modulated(xp_ref[0]).astype(BF16), hb, modulated(xn_ref[0]).astype(BF16)], axis=0)
    row = lax.broadcasted_iota(jnp.int32, (tm + 2 * halo, 1), 0)
    in_seq = jnp.logical_and(jnp.logical_or(row >= halo, i > 0),
                             jnp.logical_or(row < halo + tm, i < n_t - 1))
    off_xbc, _ = _W_OFF["xbc"]
    for j in range(0, C_XBC, 256):
        cols = slice(j, j + 256)
        ext = jnp.dot(ext_lhs, w_ref[:, off_xbc + j:off_xbc + j + 256], preferred_element_type=F32)
        ext_ref[:, cols] = jnp.where(in_seq, ext, 0.0)
        acc = jnp.zeros((tm, 256), F32) + cb_ref[:, cols]
        for k in range(C_CONV):
            start = halo + k - C_CONV // 2
            acc = acc + ext_ref[start:start + tm, cols] * cw_ref[k:k + 1, cols]
        xbc_ref[0, :, cols] = _silu(acc).astype(BF16)
    for j in range(0, C_INNER, 256):
        zc_ref[0, :, j:j + 256] = _silu(proj("zc", j, j + 256)).astype(BF16)
    for j in range(0, N_BRANCH * D_MODEL, 512):
        mg_ref[0, :, j:j + 512] = jax.nn.sigmoid(proj("mg", j, j + 512) + bg_ref[:, j:j + 512]).astype(BF16)


def _in_projection(x, sc1p, shift, norm_w, cos_t, sin_t, head_w, b_gate, conv_w, conv_b, w_main, w_t, tm=256):
    bsz, s, d = x.shape
    nt = s // tm
    hb = tm // CONV_HALO
    n_h = s // CONV_HALO
    row = lambda i, w: pl.BlockSpec((1, tm, w), lambda b, i: (b, i, 0))
    const2 = lambda shp: pl.BlockSpec(shp, lambda b, i: (0, 0))
    phase = [("phase", dil) for _ in range(3) for dil in B_DILATIONS]
    outs = ([("qa", A_Q, BF16), ("ka", A_KV, BF16), None, ("ga", A_Q, BF16)] + phase +
            [("gb", B_OUT, BF16), ("xbc", C_XBC, BF16),
             ("zc", C_INNER, BF16), ("dt", LANES, F32), ("mg", N_BRANCH * D_MODEL, BF16)])
    out_shape, out_specs = [], []
    for o in outs:
        if o is None:
            out_shape.append(jax.ShapeDtypeStruct((bsz, A_KV, s), BF16))
            out_specs.append(pl.BlockSpec((1, A_KV, tm), lambda b, i: (b, 0, i)))
        elif o[0] == "phase":
            out_shape.append(jax.ShapeDtypeStruct((bsz, o[1], s // o[1], B_OUT), BF16))
            out_specs.append(pl.BlockSpec((1, o[1], tm // o[1], B_OUT), lambda b, i: (b, 0, i, 0)))
        else:
            out_shape.append(jax.ShapeDtypeStruct((bsz, s, o[1]), o[2]))
            out_specs.append(row(0, o[1]))
    return pl.pallas_call(
        _inproj_kernel,
        grid=(bsz, nt),
        in_specs=[row(0, d),
                  pl.BlockSpec((1, CONV_HALO, d), lambda b, i: (b, jnp.maximum(i * hb - 1, 0), 0)),
                  pl.BlockSpec((1, CONV_HALO, d), lambda b, i: (b, jnp.minimum((i + 1) * hb, n_h - 1), 0)),
                  pl.BlockSpec((1, 1, d), lambda b, i: (b, 0, 0)),
                  pl.BlockSpec((1, 1, d), lambda b, i: (b, 0, 0)),
                  const2((1, d)),
                  pl.BlockSpec((tm, LANES), lambda b, i: (i, 0)),
                  pl.BlockSpec((tm, LANES), lambda b, i: (i, 0)),
                  const2((8, LANES)),
                  const2(b_gate.shape),
                  const2(conv_w.shape),
                  const2(conv_b.shape),
                  const2(w_main.shape),
                  const2(w_t.shape)],
        out_specs=out_specs,
        out_shape=out_shape,
        scratch_shapes=[pltpu.VMEM((d // LANES, tm, LANES), F32),
                        pltpu.VMEM((tm + 2 * CONV_HALO, C_XBC), F32)],
        compiler_params=_cparams(("arbitrary", "arbitrary")),
        name="norm_inproj",
    )(x, x, x, sc1p, shift, norm_w, cos_t, sin_t, head_w, b_gate, conv_w, conv_b, w_main, w_t)


A_AUG = HEAD_DIM + 8


def _attn_a_kernel(bounded_ref, q_ref, k_ref, vt_ref, o_ref, qt_ref, m_ref, acc_ref, st_ref, *, kc):
    tq = q_ref.shape[1]
    s_len = k_ref.shape[1]
    n_blk = A_Q // LANES
    lane = lax.broadcasted_iota(jnp.int32, (tq, LANES), 1)
    for j in range(n_blk):
        qblk = q_ref[0, :, j * LANES:(j + 1) * LANES]
        for kv in range(A_KV_HEADS):
            in_half = (lane < HEAD_DIM) if kv == 0 else (lane >= HEAD_DIM)
            qt_ref[A_KV_HEADS * j + kv] = jnp.where(in_half, qblk, jnp.zeros_like(qblk))
    m_ref[...] = jnp.full(m_ref.shape, NEG_INF, F32)
    acc_ref[...] = jnp.zeros(acc_ref.shape, F32)
    ones = jnp.ones((A_AUG - HEAD_DIM, kc), BF16)

    n_chunks = s_len // kc

    def scores(c, h):
        off = pl.multiple_of(c * kc, kc)
        return lax.dot_general(k_ref[0, pl.ds(off, kc), :], qt_ref[h], (((1,), (1,)), ((), ())),
                               preferred_element_type=F32)

    ahead = st_ref.shape[0]

    def sweep(online, unroll):
        for i in range(ahead):
            st_ref[i] = scores(0, i)

        def vaug(c, kv):
            off = pl.multiple_of(c * kc, kc)
            return jnp.concatenate([vt_ref[0, kv * HEAD_DIM:(kv + 1) * HEAD_DIM, pl.ds(off, kc)], ones], axis=0)

        def body(cc, carry):
            pending = [st_ref[i] for i in range(ahead)]
            for t in range(unroll * A_HEADS):
                c, h = cc * unroll + t // A_HEADS, t % A_HEADS
                nxt = t + ahead
                c_nxt = jnp.minimum(cc * unroll + nxt // A_HEADS, n_chunks - 1)
                pending.append(scores(c_nxt, nxt % A_HEADS))
                st = pending.pop(0)
                v = vaug(c, h % A_KV_HEADS)
                if online:
                    m = m_ref[h:h + 1, :]
                    m_new = jnp.maximum(m, jnp.max(st, axis=0, keepdims=True))
                    alpha = jnp.exp2(m - m_new)
                    p = jnp.exp2(st - m_new).astype(BF16)
                    acc_ref[h] = alpha * acc_ref[h] + jnp.dot(v, p, preferred_element_type=F32)
                    m_ref[h:h + 1, :] = m_new
                else:
                    p = jnp.exp2(st).astype(BF16)
                    acc_ref[h] += jnp.dot(v, p, preferred_element_type=F32)
            for i in range(ahead):
                st_ref[i] = pending[i]
            return carry

        lax.fori_loop(0, n_chunks // unroll, body, 0)

    bounded = bounded_ref[0] == 1
    pl.when(bounded)(lambda: sweep(False, 8))
    pl.when(jnp.logical_not(bounded))(lambda: sweep(True, 1))

    for j in range(n_blk):
        halves = []
        for kv in range(A_KV_HEADS):
            a = acc_ref[A_KV_HEADS * j + kv]
            halves.append(a[:HEAD_DIM] / a[HEAD_DIM:HEAD_DIM + 1])
        o_ref[0, :, j * LANES:(j + 1) * LANES] = jnp.concatenate(halves, axis=0).T.astype(o_ref.dtype)


A_SCORE_LIMIT = 40.0


def _score_bound(q_norm, k_norm):
    return (1.05 * HEAD_DIM * HEAD_DIM ** -0.5 * LOG2E) * jnp.max(jnp.abs(q_norm)) * jnp.max(jnp.abs(k_norm))


def _mixer_a(qa, ka, vt, bound, tq=256, kc=512):
    bsz, s, _ = qa.shape
    bounded = (bound <= A_SCORE_LIMIT).astype(jnp.int32).reshape(1)
    return pl.pallas_call(
        functools.partial(_attn_a_kernel, kc=kc),
        grid=(bsz, s // tq),
        in_specs=[pl.BlockSpec(memory_space=pltpu.SMEM),
                  pl.BlockSpec((1, tq, A_Q), lambda b, i: (b, i, 0)),
                  pl.BlockSpec((1, s, A_KV), lambda b, i: (b, 0, 0)),
                  pl.BlockSpec((1, A_KV, s), lambda b, i: (b, 0, 0))],
        out_specs=pl.BlockSpec((1, tq, A_Q), lambda b, i: (b, i, 0)),
        out_shape=jax.ShapeDtypeStruct((bsz, s, A_Q), BF16),
        scratch_shapes=[pltpu.VMEM((A_HEADS, tq, LANES), BF16),
                        pltpu.VMEM((A_HEADS, tq), F32),
                        pltpu.VMEM((A_HEADS, A_AUG, tq), F32),
                        pltpu.VMEM((3, kc, tq), F32)],
        compiler_params=_cparams(("arbitrary", "arbitrary")),
        name="mixer_a_attn",
    )(bounded, qa, ka, vt)


B_SIDE = 64
B_TQ = 256
B_BAND = B_TQ + 2 * B_SIDE
B_STEP = 4 * B_TQ


def _attn_b_kernel(bounded_ref, q_ref, kp_ref, kc_ref, kn_ref, vp_ref, vc_ref, vn_ref, bias_ref, o_ref, lse_ref):
    bounded = bounded_ref[0] == 1
    pl.when(bounded)(functools.partial(
        _attn_b_body, q_ref, kp_ref, kc_ref, kn_ref, vp_ref, vc_ref, vn_ref, bias_ref, o_ref, lse_ref, False))
    pl.when(jnp.logical_not(bounded))(functools.partial(
        _attn_b_body, q_ref, kp_ref, kc_ref, kn_ref, vp_ref, vc_ref, vn_ref, bias_ref, o_ref, lse_ref, True))


def _attn_b_body(q_ref, kp_ref, kc_ref, kn_ref, vp_ref, vc_ref, vn_ref, bias_ref, o_ref, lse_ref, use_max):
    mi = pl.program_id(2)
    n_m = pl.num_programs(2)
    n_sub = q_ref.shape[2] // B_TQ
    kall = jnp.concatenate([kp_ref[0, 0], kc_ref[0, 0], kn_ref[0, 0]], axis=0)
    vall = jnp.concatenate([vp_ref[0, 0], vc_ref[0, 0], vn_ref[0, 0]], axis=0)
    kk = lax.broadcasted_iota(jnp.int32, (1, B_BAND), 1)
    lane = lax.broadcasted_iota(jnp.int32, (B_TQ, B_OUT), 1) // HEAD_DIM
    for t in range(n_sub):
        q = q_ref[0, 0, t * B_TQ:(t + 1) * B_TQ, :]
        kband = kall[t * B_TQ:t * B_TQ + B_BAND]
        vband = vall[t * B_TQ:t * B_TQ + B_BAND]
        key_ok = None
        if t == 0:
            key_ok = jnp.logical_or(kk >= B_SIDE, mi > 0)
        if t == n_sub - 1:
            hi_ok = jnp.logical_or(kk < B_SIDE + B_TQ, mi < n_m - 1)
            key_ok = hi_ok if key_ok is None else jnp.logical_and(key_ok, hi_ok)
        o_acc = jnp.zeros((B_TQ, B_OUT), F32)
        lse_acc = jnp.zeros((B_TQ, B_OUT), F32)
        scores = [lax.dot_general(jnp.where(lane == h, q, jnp.zeros_like(q)), kband, (((1,), (1,)), ((), ())),
                                  preferred_element_type=F32) for h in range(B_HEADS_PER_GROUP)]
        for h in range(B_HEADS_PER_GROUP):
            s = scores[h] + bias_ref[h]
            if key_ok is not None:
                s = jnp.where(key_ok, s, NEG_INF)
            if use_max:
                mx = jnp.max(s, axis=-1, keepdims=True)
                e = jnp.exp(s - mx)
            else:
                e = jnp.exp(s)
            den = jnp.sum(e, axis=-1, keepdims=True)
            p = (e / den).astype(BF16)
            oh = jnp.dot(p, vband, preferred_element_type=F32)
            o_acc = jnp.where(lane == h, oh, o_acc)
            lse = mx + jnp.log(den) if use_max else jnp.log(den)
            lse_acc = jnp.where(lane == h, lse, lse_acc)
        o_ref[0, 0, t * B_TQ:(t + 1) * B_TQ, :] = o_acc.astype(o_ref.dtype)
        lse_ref[0, 0, t * B_TQ:(t + 1) * B_TQ, :] = lse_acc


B_SCORE_LIMIT = 60.0


def _mixer_b_group(qb, kb, vb, bias_g, bound, g):
    bsz, d, m, _ = qb.shape
    bounded = (bound <= B_SCORE_LIMIT).astype(jnp.int32).reshape(1)
    step = min(B_STEP, m)
    assert m % step == 0 and step % B_TQ == 0, (m, step)
    n_m = m // step
    hb = step // B_SIDE
    n_side = m // B_SIDE
    cur = pl.BlockSpec((1, 1, step, B_OUT), lambda b, r, i: (b, r, i, 0))
    prev = pl.BlockSpec((1, 1, B_SIDE, B_OUT), lambda b, r, i: (b, r, jnp.maximum(i * hb - 1, 0), 0))
    nxt = pl.BlockSpec((1, 1, B_SIDE, B_OUT),
                       lambda b, r, i: (b, r, jnp.minimum((i + 1) * hb, n_side - 1), 0))
    return pl.pallas_call(
        _attn_b_kernel,
        grid=(bsz, d, n_m),
        in_specs=[pl.BlockSpec(memory_space=pltpu.SMEM), cur, prev, cur, nxt, prev, cur, nxt,
                  pl.BlockSpec((B_HEADS_PER_GROUP, B_TQ, B_BAND), lambda b, r, i: (0, 0, 0))],
        out_specs=[cur, cur],
        out_shape=[jax.ShapeDtypeStruct((bsz, d, m, B_OUT), BF16),
                   jax.ShapeDtypeStruct((bsz, d, m, B_OUT), F32)],
        compiler_params=_cparams(("arbitrary", "arbitrary", "arbitrary")),
        name=f"mixer_b_attn_g{g}",
    )(bounded, qb, kb, kb, kb, vb, vb, vb, bias_g)


def _t5_bucket(rel):
    nb = N_BUCKETS // 2
    max_exact = nb // 2
    ret = jnp.where(rel > 0, nb, 0)
    n = jnp.abs(rel)
    nf = jnp.maximum(n, 1).astype(F32)
    large = max_exact + (jnp.log(nf / max_exact) / math.log(MAX_DISTANCE / max_exact)
                         * (nb - max_exact)).astype(jnp.int32)
    large = jnp.minimum(large, nb - 1)
    return ret + jnp.where(n < max_exact, n, large)


def _band_bias(rel_bias):
    rel = jnp.arange(-B_SIDE, B_SIDE + 1)
    n_rel = 2 * B_SIDE + 1
    width = B_TQ + B_BAND
    nh = B_HEADS_PER_GROUP
    tables = []
    for g in range(B_GROUPS):
        tbl = rel_bias[:, g * nh:(g + 1) * nh]
        b1 = tbl[_t5_bucket(rel * B_DILATIONS[g])].T.astype(F32)
        w = jnp.concatenate([b1, jnp.full((nh, width - n_rel), NEG_INF, F32)], axis=1)
        flat = jnp.broadcast_to(w[:, None, :], (nh, B_TQ, width)).reshape(nh, B_TQ * width)
        tables.append(flat[:, :B_TQ * (width - 1)].reshape(nh, B_TQ, width - 1)[:, :, :B_BAND])
    return jnp.stack(tables, axis=0)


CONV_HALO = 8


def _split3(x):
    hi = x.astype(BF16)
    r1 = x - hi.astype(F32)
    mid = r1.astype(BF16)
    lo = (r1 - mid.astype(F32)).astype(BF16)
    return hi, mid, lo


def _ssd_direction(x_ref, dt_ref, prm_ref, state_ref, y_ref, *, reverse):
    L = C_CHUNK
    col0 = C_HEADS if reverse else 0
    r = lax.broadcasted_iota(jnp.int32, (L, L), 0)
    c = lax.broadcasted_iota(jnp.int32, (L, L), 1)
    causal = (r <= c) if reverse else (r >= c)
    tri = jnp.where(causal, 1.0, 0.0).astype(BF16)

    groups = []
    for g in range(C_GROUPS):
        lanes = slice(g * 256, (g + 1) * 256)
        xg = x_ref[0, :, lanes].astype(F32)
        bm = x_ref[0, :, C_INNER + g * C_STATE:C_INNER + (g + 1) * C_STATE]
        cm = x_ref[0, :, C_INNER + C_BC + g * C_STATE:C_INNER + C_BC + (g + 1) * C_STATE]
        cb = lax.dot_general(cm, bm, (((1,), (1,)), ((), ())), preferred_element_type=F32)
        state = state_ref[g]
        y_in = jnp.dot(cm, state.astype(BF16), preferred_element_type=F32)
        groups.append((lanes, xg, bm, cb, state, y_in))
    yield

    dt_bias = prm_ref[0:1, :]
    a_neg = -jnp.exp(prm_ref[1:2, :])
    dtv = jax.nn.softplus(dt_ref[0] + dt_bias)
    a = dtv * a_neg
    hi, mid, lo = _split3(a)
    cs2 = jnp.dot(tri, jnp.concatenate([hi, mid], axis=1), preferred_element_type=F32)
    cs = cs2[:, :LANES] + cs2[:, LANES:] + jnp.dot(tri, lo, preferred_element_type=F32)
    yield
    cs_t = cs.T
    tot = cs[0:1, :] if reverse else cs[L - 1:L, :]
    decay_in = jnp.exp(cs)
    decay_out = jnp.exp(tot - cs)
    chunk_decay = jnp.exp(tot)

    lane_head = lax.broadcasted_iota(jnp.int32, (1, 4 * C_HEAD_DIM), 1) // C_HEAD_DIM

    def expand_row(m, g):
        out = jnp.zeros((1, 4 * C_HEAD_DIM), F32)
        for j in range(4):
            cj = col0 + g * 4 + j
            out = jnp.where(lane_head == j, m[:, cj:cj + 1], out)
        return out

    sel = (lax.broadcasted_iota(jnp.int32, (LANES, C_INNER), 0)
           == col0 + lax.broadcasted_iota(jnp.int32, (LANES, C_INNER), 1) // C_HEAD_DIM)
    sel = jnp.where(sel, 1.0, 0.0).astype(BF16)

    sel2 = jnp.concatenate([sel, sel], axis=0)

    def expand(m):
        hi = m.astype(BF16)
        lo = (m - hi.astype(F32)).astype(BF16)
        return jnp.dot(jnp.concatenate([hi, lo], axis=1), sel2, preferred_element_type=F32)

    dt_e, din_e, dout_e = expand(dtv), expand(decay_in), expand(decay_out)
    yield

    for g, (lanes, xg, bm, cb, state, y_in) in enumerate(groups):
        xdt = xg * dt_e[:, lanes]
        y = y_in * din_e[:, lanes]
        xdt_b = xdt.astype(BF16)
        parts = []
        for j in range(4):
            cj = col0 + g * 4 + j
            diff = cs[:, cj:cj + 1] - cs_t[cj:cj + 1, :]
            lm = jnp.exp(jnp.where(causal, diff, NEG_INF))
            gmat = (cb * lm).astype(BF16)
            parts.append(jnp.dot(gmat, xdt_b[:, j * C_HEAD_DIM:(j + 1) * C_HEAD_DIM],
                                 preferred_element_type=F32))
        y = y + jnp.concatenate(parts, axis=1)
        if not reverse:
            y = y + xg * expand_row(prm_ref[2:3, :], g)
        y_ref[0, :, lanes] = y.astype(y_ref.dtype)
        xs = (xdt * dout_e[:, lanes]).astype(BF16)
        upd = lax.dot_general(bm, xs, (((0,), (0,)), ((), ())), preferred_element_type=F32)
        state_ref[g] = state * expand_row(chunk_decay, g) + upd
    yield


def _ssd_kernel(xf_ref, xb_ref, dtf_ref, dtb_ref, prm_ref, yf_ref, yb_ref, sf_ref, sb_ref):
    @pl.when(pl.program_id(1) == 0)
    def _():
        sf_ref[...] = jnp.zeros_like(sf_ref)
        sb_ref[...] = jnp.zeros_like(sb_ref)

    forward = _ssd_direction(xf_ref, dtf_ref, prm_ref, sf_ref, yf_ref, reverse=False)
    backward = _ssd_direction(xb_ref, dtb_ref, prm_ref, sb_ref, yb_ref, reverse=True)
    for _ in zip(forward, backward):
        pass


def _ssd(xact, dt, prm):
    bsz, s, ch = xact.shape
    n_c = s // C_CHUNK
    fwd = lambda w: pl.BlockSpec((1, C_CHUNK, w), lambda b, i: (b, i, 0))
    bwd = lambda w: pl.BlockSpec((1, C_CHUNK, w), lambda b, i: (b, n_c - 1 - i, 0))
    return pl.pallas_call(
        _ssd_kernel,
        grid=(bsz, n_c),
        in_specs=[fwd(ch), bwd(ch), fwd(LANES), bwd(LANES), pl.BlockSpec((8, LANES), lambda b, i: (0, 0))],
        out_specs=[fwd(C_INNER), bwd(C_INNER)],
        out_shape=[jax.ShapeDtypeStruct((bsz, s, C_INNER), BF16)] * 2,
        scratch_shapes=[pltpu.VMEM((C_GROUPS, C_STATE, 4 * C_HEAD_DIM), F32)] * 2,
        compiler_params=_cparams(("arbitrary", "arbitrary")),
        name="mixer_c_ssd",
    )(xact, xact, dt, dt, prm)


def _merge_kernel(x_ref, gate_ref, oa_ref, ga_ref, ob0_ref, ob1_ref, ob2_ref, ls0_ref, ls1_ref, ls2_ref,
                  gb_ref, yf_ref, yb_ref, zc_ref, mg_ref, nw_ref, wa_ref, wb_ref, wc_ref, wo_ref,
                  o_ref, il_ref):
    ya = (oa_ref[0].astype(F32) * ga_ref[0].astype(F32)).astype(BF16)

    def token_order(ref, slot):
        d, n = ref.shape[1], ref.shape[2]
        if d == 1:
            return ref[0, 0].astype(F32)
        halves = B_OUT // LANES
        for r in range(d):
            for j in range(halves):
                il_ref[slot * halves + j, pl.ds(r, n, stride=d), :] = (
                    ref[0, r, :, j * LANES:(j + 1) * LANES].astype(F32))
        return jnp.concatenate([il_ref[slot * halves + j] for j in range(halves)], axis=1)

    l0, l1, l2 = token_order(ls0_ref, 0), token_order(ls1_ref, 0), token_order(ls2_ref, 1)
    o0, o1, o2 = token_order(ob0_ref, 0), token_order(ob1_ref, 2), token_order(ob2_ref, 3)
    mx = jnp.maximum(jnp.maximum(l0, l1), l2)
    e0, e1, e2 = jnp.exp(l0 - mx), jnp.exp(l1 - mx), jnp.exp(l2 - mx)
    yb = (e0 * o0 + e1 * o1 + e2 * o2) / (e0 + e1 + e2)
    yb = (yb * gb_ref[0].astype(F32)).astype(BF16)
    y = (yf_ref[0].astype(F32) + yb_ref[0].astype(F32)) * zc_ref[0].astype(F32)
    yc = ((y * lax.rsqrt(jnp.mean(y * y, axis=-1, keepdims=True) + EPS)) * nw_ref[...]).astype(BF16)

    d = D_MODEL
    gates = lambda k: mg_ref[0, :, k * d:(k + 1) * d].astype(F32)
    merged = gates(0) * jnp.dot(ya, wa_ref[...], preferred_element_type=F32)
    merged = merged + gates(1) * jnp.dot(yb, wb_ref[...], preferred_element_type=F32)
    merged = merged + gates(2) * jnp.dot(yc, wc_ref[...], preferred_element_type=F32)
    out = jnp.dot(merged.astype(BF16), wo_ref[...], preferred_element_type=F32)
    o_ref[0] = x_ref[0] + gate_ref[0] * out


def _merge(x, gate, oa, ga, ob, lse, gb, yf, yb, zc, mg, ssm_w, wa, wb, wc, wo, tm=512):
    bsz, s, d = x.shape
    row = lambda w: pl.BlockSpec((1, tm, w), lambda b, i: (b, i, 0))
    phase = [pl.BlockSpec((1, dil, tm // dil, B_OUT), lambda b, i: (b, 0, i, 0)) for dil in B_DILATIONS]
    const2 = lambda shp: pl.BlockSpec(shp, lambda b, i: (0, 0))
    return pl.pallas_call(
        _merge_kernel,
        grid=(bsz, s // tm),
        in_specs=[row(d), pl.BlockSpec((1, 1, d), lambda b, i: (b, 0, 0)),
                  row(A_Q), row(A_Q)] + phase + phase +
                 [row(B_OUT), row(C_INNER), row(C_INNER), row(C_INNER),
                  row(N_BRANCH * d), const2((1, C_INNER)),
                  const2(wa.shape), const2(wb.shape), const2(wc.shape), const2(wo.shape)],
        out_specs=row(d),
        out_shape=jax.ShapeDtypeStruct((bsz, s, d), F32),
        scratch_shapes=[pltpu.VMEM((4 * B_OUT // LANES, tm, LANES), F32)],
        compiler_params=_cparams(("arbitrary", "arbitrary")),
        name="merge_out",
    )(x, gate, oa, ga, *ob, *lse, gb, yf, yb, zc, mg, ssm_w, wa, wb, wc, wo)


def _rope_tables(s):
    quarter = HEAD_DIM // 4
    freqs = ROPE_THETA ** (-jnp.arange(quarter, dtype=F32) / quarter)
    t = jnp.arange(s)
    ang_r = (t // GRID_W).astype(F32)[:, None] * freqs
    ang_c = (t % GRID_W).astype(F32)[:, None] * freqs
    cos = jnp.concatenate([jnp.cos(ang_r)] * 2 + [jnp.cos(ang_c)] * 2, axis=-1)
    sin = jnp.concatenate([-jnp.sin(ang_r), jnp.sin(ang_r), -jnp.sin(ang_c), jnp.sin(ang_c)], axis=-1)
    return jnp.tile(cos, (1, LANES // HEAD_DIM)), jnp.tile(sin, (1, LANES // HEAD_DIM))


def _layer_weights(w_in_l):
    cols = lambda k: w_in_l[:, SPLIT_OFFSETS[k]:SPLIT_OFFSETS[k + 1]]
    d = w_in_l.shape[0]
    perm = lambda w: w.reshape(d, A_HEADS, HEAD_DIM)[:, A_HEAD_ORDER, :].reshape(d, A_Q)
    dt_cols = jnp.concatenate([cols(12), cols(13), jnp.zeros((d, LANES - 2 * C_HEADS), w_in_l.dtype)], axis=1)
    pieces = {"qa": perm(cols(0)), "ka": cols(1), "ga": perm(cols(3)), "qb": cols(4), "kb": cols(5),
              "vb": cols(6), "gb": cols(7), "xbc": jnp.concatenate([cols(8), cols(10), cols(11)], axis=1),
              "zc": cols(9), "dt": dt_cols, "mg": cols(14)}
    w_main = jnp.concatenate([pieces[n] for n, _ in _W_GROUPS], axis=1).astype(BF16)
    w_t = cols(2).T.astype(BF16)
    return w_main, w_t


def _pad_row(v, width=LANES):
    return jnp.pad(v.astype(F32), (0, width - v.shape[0]))


def kernel(x, c, norm_w, w_ada, b_ada, w_in, b_gate, q_norm_a, k_norm_a, q_norm_b, k_norm_b, rel_bias,
           conv_w, conv_b, a_log, dt_bias, d_skip, ssm_norm_w, w_proj_a, w_proj_b, w_proj_c, w_out):
    bsz, s, d = x.shape
    depth = w_in.shape[0]
    c_pad = jnp.pad(c, ((0, 8 - bsz), (0, 0)))
    mod = _modulation(c_pad, w_ada, b_ada)[:, :bsz]
    cos_t, sin_t = _rope_tables(s)
    bias_tbl = _band_bias(rel_bias)
    tile2 = lambda v: jnp.tile(v.astype(F32), LANES // HEAD_DIM)
    for l in range(depth):
        shift, scale, gate = (mod[l, :, k * d:(k + 1) * d].reshape(bsz, 1, d) for k in range(3))
        w_main, w_t = _layer_weights(w_in[l])
        head_w = jnp.stack([tile2(q_norm_a[l]), tile2(k_norm_a[l]), tile2(q_norm_b[l]), tile2(k_norm_b[l])]
                           + [jnp.zeros((LANES,), F32)] * 4, axis=0)
        (qa, ka, vt, ga, *qkv_b, gb, xbc, zc, dt, mg) = _in_projection(
            x, 1.0 + scale, shift, norm_w[l].reshape(1, d), cos_t, sin_t, head_w,
            b_gate[l].reshape(1, -1), jnp.pad(conv_w[l], ((0, 8 - C_CONV), (0, 0))),
            conv_b[l].reshape(1, C_XBC), w_main, w_t)
        oa = _mixer_a(qa, ka, vt, _score_bound(q_norm_a[l], k_norm_a[l]))
        bound_b = (1.05 * HEAD_DIM ** 0.5 * jnp.max(jnp.abs(q_norm_b[l])) * jnp.max(jnp.abs(k_norm_b[l]))
                   + jnp.max(jnp.abs(rel_bias)))
        ob, lse = zip(*[_mixer_b_group(qkv_b[g], qkv_b[B_GROUPS + g], qkv_b[2 * B_GROUPS + g], bias_tbl[g],
                                       bound_b, g) for g in range(B_GROUPS)])
        prm = jnp.stack([_pad_row(dt_bias[l].reshape(-1)), _pad_row(a_log[l].reshape(-1)),
                         _pad_row(jnp.tile(d_skip[l], 2))] + [jnp.zeros((LANES,), F32)] * 5, axis=0)
        yf, yb = _ssd(xbc, dt, prm)
        perm_rows = lambda w: w.reshape(A_HEADS, HEAD_DIM, d)[A_HEAD_ORDER, :, :].reshape(A_Q, d)
        x = _merge(x, gate, oa, ga, ob, lse, gb, yf, yb, zc, mg, ssm_norm_w[l].reshape(1, -1),
                   perm_rows(w_proj_a[l]).astype(BF16), w_proj_b[l].astype(BF16),
                   w_proj_c[l].astype(BF16), w_out[l].astype(BF16))
    return x
```

```python
import functools
import math

import numpy as np
import jax
import jax.numpy as jnp
from jax import lax
from jax.experimental import pallas as pl
from jax.experimental.pallas import tpu as pltpu

F32 = jnp.float32
BF16 = jnp.bfloat16

D_MODEL = 1024
HEAD_DIM = 64
EPS = 1e-6
NEG_INF = -1e30
GRID_W = 64
A_HEADS = 8
A_KV_HEADS = 2
ROPE_THETA = 10000.0
B_GROUPS = 3
B_HEADS_PER_GROUP = 4
B_WINDOWS = (128, 512, 2048)
B_DILATIONS = (1, 4, 16)
N_BUCKETS = 32
MAX_DISTANCE = 1024
C_HEADS = 8
C_HEAD_DIM = 64
C_INNER = C_HEADS * C_HEAD_DIM
C_GROUPS = 2
C_STATE = 128
C_CONV = 5
C_CHUNK = 128
N_BRANCH = 3

A_Q = A_HEADS * HEAD_DIM
A_KV = A_KV_HEADS * HEAD_DIM
B_QKV = B_GROUPS * B_HEADS_PER_GROUP * HEAD_DIM
B_OUT = B_HEADS_PER_GROUP * HEAD_DIM
C_BC = C_GROUPS * C_STATE
C_XBC = C_INNER + 2 * C_BC
SPLIT_WIDTHS = (A_Q, A_KV, A_KV, A_Q, B_QKV, B_QKV, B_QKV, B_OUT,
                C_INNER, C_INNER, C_BC, C_BC, C_HEADS, C_HEADS, N_BRANCH * D_MODEL)
SPLIT_OFFSETS = tuple(int(v) for v in np.cumsum((0,) + SPLIT_WIDTHS))

LANES = 128
VMEM_LIMIT = 56 * 1024 * 1024
LOG2E = math.log2(math.e)

_W_GROUPS = (("qa", A_Q), ("ka", A_KV), ("dt", LANES), ("ga", A_Q), ("qb", B_QKV), ("kb", B_QKV), ("vb", B_QKV),
             ("gb", B_OUT), ("xbc", C_XBC), ("zc", C_INNER), ("mg", N_BRANCH * D_MODEL))
MXU_COLS = 256
_W_OFF = {}
_o = 0
for _n, _w in _W_GROUPS:
    _W_OFF[_n] = (_o, _w)
    _o += _w
W_MAIN_COLS = _o

A_HEAD_ORDER = tuple(h for j in range(4) for h in (j, j + 4))


def _cparams(sem):
    return pltpu.CompilerParams(dimension_semantics=sem, vmem_limit_bytes=VMEM_LIMIT)


def _silu(x):
    return x * jax.nn.sigmoid(x)


def _mod_kernel(c_ref, w_ref, b_ref, o_ref):
    c = c_ref[...]
    o_ref[0] = jnp.dot(_silu(c), w_ref[0], preferred_element_type=F32,
                       precision=lax.Precision.HIGHEST) + b_ref[0]


def _modulation(c_pad, w_ada, b_ada):
    depth, d, n = w_ada.shape
    tn = 512
    return pl.pallas_call(
        _mod_kernel,
        grid=(depth, n // tn),
        in_specs=[pl.BlockSpec((c_pad.shape[0], d), lambda l, j: (0, 0)),
                  pl.BlockSpec((1, d, tn), lambda l, j: (l, 0, j)),
                  pl.BlockSpec((1, 1, tn), lambda l, j: (l, 0, j))],
        out_specs=pl.BlockSpec((1, c_pad.shape[0], tn), lambda l, j: (l, 0, j)),
        out_shape=jax.ShapeDtypeStruct((depth, c_pad.shape[0], n), F32),
        compiler_params=_cparams(("arbitrary", "arbitrary")),
        name="adaln_mod",
    )(c_pad, w_ada, b_ada.reshape(depth, 1, n))


def _head_sumsq_matrix():
    r = lax.broadcasted_iota(jnp.int32, (MXU_COLS, MXU_COLS), 0) // HEAD_DIM
    c = lax.broadcasted_iota(jnp.int32, (MXU_COLS, MXU_COLS), 1) // HEAD_DIM
    return jnp.where(r == c, 1.0 / HEAD_DIM, 0.0).astype(BF16)


def _inproj_kernel(x_ref, xp_ref, xn_ref, sc_ref, sh_ref, nw_ref, cos_ref, sin_ref, hw_ref, bg_ref,
                   cw_ref, cb_ref, w_ref, wt_ref,
                   qa_ref, ka_ref, vt_ref, ga_ref,
                   qb0_ref, qb1_ref, qb2_ref, kb0_ref, kb1_ref, kb2_ref, vb0_ref, vb1_ref, vb2_ref,
                   gb_ref, xbc_ref, zc_ref, dt_ref, mg_ref, h_ref, ext_ref):
    tm = x_ref.shape[1]

    def modulated(xv):
        ms = jnp.mean(xv * xv, axis=-1, keepdims=True)
        return ((xv * lax.rsqrt(ms + EPS)) * nw_ref[...]) * sc_ref[0] + sh_ref[0]

    h = modulated(x_ref[0])
    hb = h.astype(BF16)
    for j in range(D_MODEL // LANES):
        h_ref[j] = h[:, j * LANES:(j + 1) * LANES]

    bd = _head_sumsq_matrix()
    lane = lax.broadcasted_iota(jnp.int32, (1, LANES), 1)
    first_half = (lane % 32) < 16
    cos = cos_ref[...]
    sin = sin_ref[...]

    def proj(name, c0, c1):
        off, _ = _W_OFF[name]
        return jnp.dot(hb, w_ref[:, off + c0:off + c1], preferred_element_type=F32)

    def head_norm(y, w_row):
        n = y.shape[1]
        ss = jnp.dot((y * y).astype(BF16), bd[:n, :n], preferred_element_type=F32)
        return (y * lax.rsqrt(ss + EPS)) * jnp.concatenate([w_row] * (n // LANES), axis=1)

    def rope(y):
        sw = jnp.where(first_half, pltpu.roll(y, LANES - 16, 1), pltpu.roll(y, 16, 1))
        return y * cos + sw * sin

    deferred = []

    def normed(y, finish):
        waiting = list(deferred)
        deferred[:] = [(y, finish)]
        for py, pfinish in waiting:
            pfinish(py)

    def finish_qa(y, c0):
        y = head_norm(y, hw_ref[0:1, :])
        for j in range(0, y.shape[1], LANES):
            blk = rope(y[:, j:j + LANES]) * (HEAD_DIM ** -0.5 * LOG2E)
            qa_ref[0, :, c0 + j:c0 + j + LANES] = blk.astype(BF16)

    def finish_ka(y):
        ka_ref[0] = rope(head_norm(y, hw_ref[1:2, :])).astype(BF16)

    for c0 in range(0, A_Q, MXU_COLS):
        normed(proj("qa", c0, c0 + MXU_COLS), functools.partial(finish_qa, c0=c0))
    ka_dt = proj("ka", 0, A_KV + LANES)
    dt_ref[0] = ka_dt[:, A_KV:]
    normed(ka_dt[:, :A_KV], finish_ka)
    vt_ref[0] = lax.dot_general(wt_ref[...], hb, (((1,), (1,)), ((), ())),
                                preferred_element_type=F32).astype(BF16)
    for j in range(0, A_Q, 256):
        ga_ref[0, :, j:j + 256] = _silu(proj("ga", j, j + 256)).astype(BF16)
    def store_phases(ref, lanes, y, d):
        n = tm // d
        for r in range(d):
            ref[0, r, :, lanes] = y[r * n:(r + 1) * n, :]

    for g, (qr, kr, vr) in enumerate(((qb0_ref, kb0_ref, vb0_ref), (qb1_ref, kb1_ref, vb1_ref),
                                      (qb2_ref, kb2_ref, vb2_ref))):
        d = B_DILATIONS[g]
        if d == 1:
            lhs = hb
        else:
            lhs = jnp.concatenate(
                [jnp.concatenate([h_ref[j, pl.ds(r, tm // d, stride=d), :] for r in range(d)], axis=0)
                 for j in range(D_MODEL // LANES)], axis=1).astype(BF16)

        def proj_g(name, c0, c1, lhs=lhs):
            off, _ = _W_OFF[name]
            return jnp.dot(lhs, w_ref[:, off + c0:off + c1], preferred_element_type=F32)

        def finish_q(y, ref=qr, d=d):
            store_phases(ref, slice(0, B_OUT), (head_norm(y, hw_ref[2:3, :]) * (HEAD_DIM ** -0.5)).astype(BF16), d)

        def finish_k(y, ref=kr, d=d):
            store_phases(ref, slice(0, B_OUT), head_norm(y, hw_ref[3:4, :]).astype(BF16), d)

        normed(proj_g("qb", g * B_OUT, (g + 1) * B_OUT), finish_q)
        normed(proj_g("kb", g * B_OUT, (g + 1) * B_OUT), finish_k)
        v = proj_g("vb", g * B_OUT, (g + 1) * B_OUT)
        store_phases(vr, slice(0, B_OUT), v.astype(BF16), d)
    normed(None, None)
    deferred.clear()
    gb_ref[0] = _silu(proj("gb", 0, B_OUT)).astype(BF16)
    i = pl.program_id(1)
    n_t = pl.num_programs(1)
    halo = xp_ref.shape[1]
    ext_lhs = jnp.concatenate([modulated(xp_ref[0]).astype(BF16), hb, modulated(xn_ref[0]).astype(BF16)], axis=0)
    row = lax.broadcasted_iota(jnp.int32, (tm + 2 * halo, 1), 0)
    in_seq = jnp.logical_and(jnp.logical_or(row >= halo, i > 0),
                             jnp.logical_or(row < halo + tm, i < n_t - 1))
    off_xbc, _ = _W_OFF["xbc"]
    for j in range(0, C_XBC, 256):
        cols = slice(j, j + 256)
        ext = jnp.dot(ext_lhs, w_ref[:, off_xbc + j:off_xbc + j + 256], preferred_element_type=F32)
        ext_ref[:, cols] = jnp.where(in_seq, ext, 0.0)
        acc = jnp.zeros((tm, 256), F32) + cb_ref[:, cols]
        for k in range(C_CONV):
            start = halo + k - C_CONV // 2
            acc = acc + ext_ref[start:start + tm, cols] * cw_ref[k:k + 1, cols]
        xbc_ref[0, :, cols] = _silu(acc).astype(BF16)
    for j in range(0, C_INNER, 256):
        zc_ref[0, :, j:j + 256] = _silu(proj("zc", j, j + 256)).astype(BF16)
    for j in range(0, N_BRANCH * D_MODEL, 512):
        mg_ref[0, :, j:j + 512] = jax.nn.sigmoid(proj("mg", j, j + 512) + bg_ref[:, j:j + 512]).astype(BF16)


def _in_projection(x, sc1p, shift, norm_w, cos_t, sin_t, head_w, b_gate, conv_w, conv_b, w_main, w_t, tm=256):
    bsz, s, d = x.shape
    nt = s // tm
    hb = tm // CONV_HALO
    n_h = s // CONV_HALO
    row = lambda i, w: pl.BlockSpec((1, tm, w), lambda b, i: (b, i, 0))
    const2 = lambda shp: pl.BlockSpec(shp, lambda b, i: (0, 0))
    phase = [("phase", dil) for _ in range(3) for dil in B_DILATIONS]
    outs = ([("qa", A_Q, BF16), ("ka", A_KV, BF16), None, ("ga", A_Q, BF16)] + phase +
            [("gb", B_OUT, BF16), ("xbc", C_XBC, BF16),
             ("zc", C_INNER, BF16), ("dt", LANES, F32), ("mg", N_BRANCH * D_MODEL, BF16)])
    out_shape, out_specs = [], []
    for o in outs:
        if o is None:
            out_shape.append(jax.ShapeDtypeStruct((bsz, A_KV, s), BF16))
            out_specs.append(pl.BlockSpec((1, A_KV, tm), lambda b, i: (b, 0, i)))
        elif o[0] == "phase":
            out_shape.append(jax.ShapeDtypeStruct((bsz, o[1], s // o[1], B_OUT), BF16))
            out_specs.append(pl.BlockSpec((1, o[1], tm // o[1], B_OUT), lambda b, i: (b, 0, i, 0)))
        else:
            out_shape.append(jax.ShapeDtypeStruct((bsz, s, o[1]), o[2]))
            out_specs.append(row(0, o[1]))
    return pl.pallas_call(
        _inproj_kernel,
        grid=(bsz, nt),
        in_specs=[row(0, d),
                  pl.BlockSpec((1, CONV_HALO, d), lambda b, i: (b, jnp.maximum(i * hb - 1, 0), 0)),
                  pl.BlockSpec((1, CONV_HALO, d), lambda b, i: (b, jnp.minimum((i + 1) * hb, n_h - 1), 0)),
                  pl.BlockSpec((1, 1, d), lambda b, i: (b, 0, 0)),
                  pl.BlockSpec((1, 1, d), lambda b, i: (b, 0, 0)),
                  const2((1, d)),
                  pl.BlockSpec((tm, LANES), lambda b, i: (i, 0)),
                  pl.BlockSpec((tm, LANES), lambda b, i: (i, 0)),
                  const2((8, LANES)),
                  const2(b_gate.shape),
                  const2(conv_w.shape),
                  const2(conv_b.shape),
                  const2(w_main.shape),
                  const2(w_t.shape)],
        out_specs=out_specs,
        out_shape=out_shape,
        scratch_shapes=[pltpu.VMEM((d // LANES, tm, LANES), F32),
                        pltpu.VMEM((tm + 2 * CONV_HALO, C_XBC), F32)],
        compiler_params=_cparams(("arbitrary", "arbitrary")),
        name="norm_inproj",
    )(x, x, x, sc1p, shift, norm_w, cos_t, sin_t, head_w, b_gate, conv_w, conv_b, w_main, w_t)


A_AUG = HEAD_DIM + 8


def _attn_a_kernel(bounded_ref, q_ref, k_ref, vt_ref, o_ref, qt_ref, m_ref, acc_ref, st_ref, *, kc):
    tq = q_ref.shape[1]
    s_len = k_ref.shape[1]
    n_blk = A_Q // LANES
    lane = lax.broadcasted_iota(jnp.int32, (tq, LANES), 1)
    for j in range(n_blk):
        qblk = q_ref[0, :, j * LANES:(j + 1) * LANES]
        for kv in range(A_KV_HEADS):
            in_half = (lane < HEAD_DIM) if kv == 0 else (lane >= HEAD_DIM)
            qt_ref[A_KV_HEADS * j + kv] = jnp.where(in_half, qblk, jnp.zeros_like(qblk))
    m_ref[...] = jnp.full(m_ref.shape, NEG_INF, F32)
    acc_ref[...] = jnp.zeros(acc_ref.shape, F32)
    ones = jnp.ones((A_AUG - HEAD_DIM, kc), BF16)

    n_chunks = s_len // kc

    def scores(c, h):
        off = pl.multiple_of(c * kc, kc)
        return lax.dot_general(k_ref[0, pl.ds(off, kc), :], qt_ref[h], (((1,), (1,)), ((), ())),
                               preferred_element_type=F32)

    ahead = st_ref.shape[0]

    def sweep(online, unroll):
        for i in range(ahead):
            st_ref[i] = scores(0, i)

        def vaug(c, kv):
            off = pl.multiple_of(c * kc, kc)
            return jnp.concatenate([vt_ref[0, kv * HEAD_DIM:(kv + 1) * HEAD_DIM, pl.ds(off, kc)], ones], axis=0)

        def body(cc, carry):
            pending = [st_ref[i] for i in range(ahead)]
            for t in range(unroll * A_HEADS):
                c, h = cc * unroll + t // A_HEADS, t % A_HEADS
                nxt = t + ahead
                c_nxt = jnp.minimum(cc * unroll + nxt // A_HEADS, n_chunks - 1)
                pending.append(scores(c_nxt, nxt % A_HEADS))
                st = pending.pop(0)
                v = vaug(c, h % A_KV_HEADS)
                if online:
                    m = m_ref[h:h + 1, :]
                    m_new = jnp.maximum(m, jnp.max(st, axis=0, keepdims=True))
                    alpha = jnp.exp2(m - m_new)
                    p = jnp.exp2(st - m_new).astype(BF16)
                    acc_ref[h] = alpha * acc_ref[h] + jnp.dot(v, p, preferred_element_type=F32)
                    m_ref[h:h + 1, :] = m_new
                else:
                    p = jnp.exp2(st).astype(BF16)
                    acc_ref[h] += jnp.dot(v, p, preferred_element_type=F32)
            for i in range(ahead):
                st_ref[i] = pending[i]
            return carry

        lax.fori_loop(0, n_chunks // unroll, body, 0)

    bounded = bounded_ref[0] == 1
    pl.when(bounded)(lambda: sweep(False, 8))
    pl.when(jnp.logical_not(bounded))(lambda: sweep(True, 1))

    for j in range(n_blk):
        halves = []
        for kv in range(A_KV_HEADS):
            a = acc_ref[A_KV_HEADS * j + kv]
            halves.append(a[:HEAD_DIM] / a[HEAD_DIM:HEAD_DIM + 1])
        o_ref[0, :, j * LANES:(j + 1) * LANES] = jnp.concatenate(halves, axis=0).T.astype(o_ref.dtype)


A_SCORE_LIMIT = 40.0


def _score_bound(q_norm, k_norm):
    return (1.05 * HEAD_DIM * HEAD_DIM ** -0.5 * LOG2E) * jnp.max(jnp.abs(q_norm)) * jnp.max(jnp.abs(k_norm))


def _mixer_a(qa, ka, vt, bound, tq=256, kc=512):
    bsz, s, _ = qa.shape
    bounded = (bound <= A_SCORE_LIMIT).astype(jnp.int32).reshape(1)
    return pl.pallas_call(
        functools.partial(_attn_a_kernel, kc=kc),
        grid=(bsz, s // tq),
        in_specs=[pl.BlockSpec(memory_space=pltpu.SMEM),
                  pl.BlockSpec((1, tq, A_Q), lambda b, i: (b, i, 0)),
                  pl.BlockSpec((1, s, A_KV), lambda b, i: (b, 0, 0)),
                  pl.BlockSpec((1, A_KV, s), lambda b, i: (b, 0, 0))],
        out_specs=pl.BlockSpec((1, tq, A_Q), lambda b, i: (b, i, 0)),
        out_shape=jax.ShapeDtypeStruct((bsz, s, A_Q), BF16),
        scratch_shapes=[pltpu.VMEM((A_HEADS, tq, LANES), BF16),
                        pltpu.VMEM((A_HEADS, tq), F32),
                        pltpu.VMEM((A_HEADS, A_AUG, tq), F32),
                        pltpu.VMEM((2, kc, tq), F32)],
        compiler_params=_cparams(("arbitrary", "arbitrary")),
        name="mixer_a_attn",
    )(bounded, qa, ka, vt)


B_SIDE = 64
B_TQ = 256
B_BAND = B_TQ + 2 * B_SIDE
B_STEP = 4 * B_TQ


def _attn_b_kernel(bounded_ref, q_ref, kp_ref, kc_ref, kn_ref, vp_ref, vc_ref, vn_ref, bias_ref, o_ref, lse_ref):
    bounded = bounded_ref[0] == 1
    pl.when(bounded)(functools.partial(
        _attn_b_body, q_ref, kp_ref, kc_ref, kn_ref, vp_ref, vc_ref, vn_ref, bias_ref, o_ref, lse_ref, False))
    pl.when(jnp.logical_not(bounded))(functools.partial(
        _attn_b_body, q_ref, kp_ref, kc_ref, kn_ref, vp_ref, vc_ref, vn_ref, bias_ref, o_ref, lse_ref, True))


def _attn_b_body(q_ref, kp_ref, kc_ref, kn_ref, vp_ref, vc_ref, vn_ref, bias_ref, o_ref, lse_ref, use_max):
    mi = pl.program_id(2)
    n_m = pl.num_programs(2)
    n_sub = q_ref.shape[2] // B_TQ
    kall = jnp.concatenate([kp_ref[0, 0], kc_ref[0, 0], kn_ref[0, 0]], axis=0)
    vall = jnp.concatenate([vp_ref[0, 0], vc_ref[0, 0], vn_ref[0, 0]], axis=0)
    kk = lax.broadcasted_iota(jnp.int32, (1, B_BAND), 1)
    lane = lax.broadcasted_iota(jnp.int32, (B_TQ, B_OUT), 1) // HEAD_DIM
    for t in range(n_sub):
        q = q_ref[0, 0, t * B_TQ:(t + 1) * B_TQ, :]
        kband = kall[t * B_TQ:t * B_TQ + B_BAND]
        vband = vall[t * B_TQ:t * B_TQ + B_BAND]
        key_ok = None
        if t == 0:
            key_ok = jnp.logical_or(kk >= B_SIDE, mi > 0)
        if t == n_sub - 1:
            hi_ok = jnp.logical_or(kk < B_SIDE + B_TQ, mi < n_m - 1)
            key_ok = hi_ok if key_ok is None else jnp.logical_and(key_ok, hi_ok)
        o_acc = jnp.zeros((B_TQ, B_OUT), F32)
        lse_acc = jnp.zeros((B_TQ, B_OUT), F32)
        scores = [lax.dot_general(jnp.where(lane == h, q, jnp.zeros_like(q)), kband, (((1,), (1,)), ((), ())),
                                  preferred_element_type=F32) for h in range(B_HEADS_PER_GROUP)]
        for h in range(B_HEADS_PER_GROUP):
            s = scores[h] + bias_ref[h]
            if key_ok is not None:
                s = jnp.where(key_ok, s, NEG_INF)
            if use_max:
                mx = jnp.max(s, axis=-1, keepdims=True)
                e = jnp.exp(s - mx)
            else:
                e = jnp.exp(s)
            den = jnp.sum(e, axis=-1, keepdims=True)
            p = (e / den).astype(BF16)
            oh = jnp.dot(p, vband, preferred_element_type=F32)
            o_acc = jnp.where(lane == h, oh, o_acc)
            lse = mx + jnp.log(den) if use_max else jnp.log(den)
            lse_acc = jnp.where(lane == h, lse, lse_acc)
        o_ref[0, 0, t * B_TQ:(t + 1) * B_TQ, :] = o_acc.astype(o_ref.dtype)
        lse_ref[0, 0, t * B_TQ:(t + 1) * B_TQ, :] = lse_acc


B_SCORE_LIMIT = 60.0


def _mixer_b_group(qb, kb, vb, bias_g, bound, g):
    bsz, d, m, _ = qb.shape
    bounded = (bound <= B_SCORE_LIMIT).astype(jnp.int32).reshape(1)
    step = min(B_STEP, m)
    assert m % step == 0 and step % B_TQ == 0, (m, step)
    n_m = m // step
    hb = step // B_SIDE
    n_side = m // B_SIDE
    cur = pl.BlockSpec((1, 1, step, B_OUT), lambda b, r, i: (b, r, i, 0))
    prev = pl.BlockSpec((1, 1, B_SIDE, B_OUT), lambda b, r, i: (b, r, jnp.maximum(i * hb - 1, 0), 0))
    nxt = pl.BlockSpec((1, 1, B_SIDE, B_OUT),
                       lambda b, r, i: (b, r, jnp.minimum((i + 1) * hb, n_side - 1), 0))
    return pl.pallas_call(
        _attn_b_kernel,
        grid=(bsz, d, n_m),
        in_specs=[pl.BlockSpec(memory_space=pltpu.SMEM), cur, prev, cur, nxt, prev, cur, nxt,
                  pl.BlockSpec((B_HEADS_PER_GROUP, B_TQ, B_BAND), lambda b, r, i: (0, 0, 0))],
        out_specs=[cur, cur],
        out_shape=[jax.ShapeDtypeStruct((bsz, d, m, B_OUT), BF16),
                   jax.ShapeDtypeStruct((bsz, d, m, B_OUT), F32)],
        compiler_params=_cparams(("arbitrary", "arbitrary", "arbitrary")),
        name=f"mixer_b_attn_g{g}",
    )(bounded, qb, kb, kb, kb, vb, vb, vb, bias_g)


def _t5_bucket(rel):
    nb = N_BUCKETS // 2
    max_exact = nb // 2
    ret = jnp.where(rel > 0, nb, 0)
    n = jnp.abs(rel)
    nf = jnp.maximum(n, 1).astype(F32)
    large = max_exact + (jnp.log(nf / max_exact) / math.log(MAX_DISTANCE / max_exact)
                         * (nb - max_exact)).astype(jnp.int32)
    large = jnp.minimum(large, nb - 1)
    return ret + jnp.where(n < max_exact, n, large)


def _band_bias(rel_bias):
    rel = jnp.arange(-B_SIDE, B_SIDE + 1)
    n_rel = 2 * B_SIDE + 1
    width = B_TQ + B_BAND
    nh = B_HEADS_PER_GROUP
    tables = []
    for g in range(B_GROUPS):
        tbl = rel_bias[:, g * nh:(g + 1) * nh]
        b1 = tbl[_t5_bucket(rel * B_DILATIONS[g])].T.astype(F32)
        w = jnp.concatenate([b1, jnp.full((nh, width - n_rel), NEG_INF, F32)], axis=1)
        flat = jnp.broadcast_to(w[:, None, :], (nh, B_TQ, width)).reshape(nh, B_TQ * width)
        tables.append(flat[:, :B_TQ * (width - 1)].reshape(nh, B_TQ, width - 1)[:, :, :B_BAND])
    return jnp.stack(tables, axis=0)


CONV_HALO = 8


def _split3(x):
    hi = x.astype(BF16)
    r1 = x - hi.astype(F32)
    mid = r1.astype(BF16)
    lo = (r1 - mid.astype(F32)).astype(BF16)
    return hi, mid, lo


def _ssd_direction(x_ref, dt_ref, prm_ref, state_ref, y_ref, *, reverse):
    L = C_CHUNK
    col0 = C_HEADS if reverse else 0
    r = lax.broadcasted_iota(jnp.int32, (L, L), 0)
    c = lax.broadcasted_iota(jnp.int32, (L, L), 1)
    causal = (r <= c) if reverse else (r >= c)
    tri = jnp.where(causal, 1.0, 0.0).astype(BF16)

    groups = []
    for g in range(C_GROUPS):
        lanes = slice(g * 256, (g + 1) * 256)
        xg = x_ref[0, :, lanes].astype(F32)
        bm = x_ref[0, :, C_INNER + g * C_STATE:C_INNER + (g + 1) * C_STATE]
        cm = x_ref[0, :, C_INNER + C_BC + g * C_STATE:C_INNER + C_BC + (g + 1) * C_STATE]
        cb = lax.dot_general(cm, bm, (((1,), (1,)), ((), ())), preferred_element_type=F32)
        state = state_ref[g]
        y_in = jnp.dot(cm, state.astype(BF16), preferred_element_type=F32)
        groups.append((lanes, xg, bm, cb, state, y_in))
    yield

    dt_bias = prm_ref[0:1, :]
    a_neg = -jnp.exp(prm_ref[1:2, :])
    dtv = jax.nn.softplus(dt_ref[0] + dt_bias)
    a = dtv * a_neg
    hi, mid, lo = _split3(a)
    cs2 = jnp.dot(tri, jnp.concatenate([hi, mid], axis=1), preferred_element_type=F32)
    cs = cs2[:, :LANES] + cs2[:, LANES:] + jnp.dot(tri, lo, preferred_element_type=F32)
    yield
    cs_t = cs.T
    tot = cs[0:1, :] if reverse else cs[L - 1:L, :]
    decay_in = jnp.exp(cs)
    decay_out = jnp.exp(tot - cs)
    chunk_decay = jnp.exp(tot)

    lane_head = lax.broadcasted_iota(jnp.int32, (1, 4 * C_HEAD_DIM), 1) // C_HEAD_DIM

    def expand_row(m, g):
        out = jnp.zeros((1, 4 * C_HEAD_DIM), F32)
        for j in range(4):
            cj = col0 + g * 4 + j
            out = jnp.where(lane_head == j, m[:, cj:cj + 1], out)
        return out

    sel = (lax.broadcasted_iota(jnp.int32, (LANES, C_INNER), 0)
           == col0 + lax.broadcasted_iota(jnp.int32, (LANES, C_INNER), 1) // C_HEAD_DIM)
    sel = jnp.where(sel, 1.0, 0.0).astype(BF16)

    sel2 = jnp.concatenate([sel, sel], axis=0)

    def expand(m):
        hi = m.astype(BF16)
        lo = (m - hi.astype(F32)).astype(BF16)
        return jnp.dot(jnp.concatenate([hi, lo], axis=1), sel2, preferred_element_type=F32)

    dt_e, din_e, dout_e = expand(dtv), expand(decay_in), expand(decay_out)
    yield

    for g, (lanes, xg, bm, cb, state, y_in) in enumerate(groups):
        xdt = xg * dt_e[:, lanes]
        y = y_in * din_e[:, lanes]
        xdt_b = xdt.astype(BF16)
        parts = []
        for j in range(4):
            cj = col0 + g * 4 + j
            diff = cs[:, cj:cj + 1] - cs_t[cj:cj + 1, :]
            lm = jnp.exp(jnp.where(causal, diff, NEG_INF))
            gmat = (cb * lm).astype(BF16)
            parts.append(jnp.dot(gmat, xdt_b[:, j * C_HEAD_DIM:(j + 1) * C_HEAD_DIM],
                                 preferred_element_type=F32))
        y = y + jnp.concatenate(parts, axis=1)
        if not reverse:
            y = y + xg * expand_row(prm_ref[2:3, :], g)
        y_ref[0, :, lanes] = y.astype(y_ref.dtype)
        xs = (xdt * dout_e[:, lanes]).astype(BF16)
        upd = lax.dot_general(bm, xs, (((0,), (0,)), ((), ())), preferred_element_type=F32)
        state_ref[g] = state * expand_row(chunk_decay, g) + upd
    yield


def _ssd_kernel(xf_ref, xb_ref, dtf_ref, dtb_ref, prm_ref, yf_ref, yb_ref, sf_ref, sb_ref):
    @pl.when(pl.program_id(1) == 0)
    def _():
        sf_ref[...] = jnp.zeros_like(sf_ref)
        sb_ref[...] = jnp.zeros_like(sb_ref)

    forward = _ssd_direction(xf_ref, dtf_ref, prm_ref, sf_ref, yf_ref, reverse=False)
    backward = _ssd_direction(xb_ref, dtb_ref, prm_ref, sb_ref, yb_ref, reverse=True)
    for _ in zip(forward, backward):
        pass


def _ssd(xact, dt, prm):
    bsz, s, ch = xact.shape
    n_c = s // C_CHUNK
    fwd = lambda w: pl.BlockSpec((1, C_CHUNK, w), lambda b, i: (b, i, 0))
    bwd = lambda w: pl.BlockSpec((1, C_CHUNK, w), lambda b, i: (b, n_c - 1 - i, 0))
    return pl.pallas_call(
        _ssd_kernel,
        grid=(bsz, n_c),
        in_specs=[fwd(ch), bwd(ch), fwd(LANES), bwd(LANES), pl.BlockSpec((8, LANES), lambda b, i: (0, 0))],
        out_specs=[fwd(C_INNER), bwd(C_INNER)],
        out_shape=[jax.ShapeDtypeStruct((bsz, s, C_INNER), BF16)] * 2,
        scratch_shapes=[pltpu.VMEM((C_GROUPS, C_STATE, 4 * C_HEAD_DIM), F32)] * 2,
        compiler_params=_cparams(("arbitrary", "arbitrary")),
        name="mixer_c_ssd",
    )(xact, xact, dt, dt, prm)


def _merge_kernel(x_ref, gate_ref, oa_ref, ga_ref, ob0_ref, ob1_ref, ob2_ref, ls0_ref, ls1_ref, ls2_ref,
                  gb_ref, yf_ref, yb_ref, zc_ref, mg_ref, nw_ref, wa_ref, wb_ref, wc_ref, wo_ref,
                  o_ref, il_ref):
    ya = (oa_ref[0].astype(F32) * ga_ref[0].astype(F32)).astype(BF16)

    def token_order(ref, slot):
        d, n = ref.shape[1], ref.shape[2]
        if d == 1:
            return ref[0, 0].astype(F32)
        halves = B_OUT // LANES
        for r in range(d):
            for j in range(halves):
                il_ref[slot * halves + j, pl.ds(r, n, stride=d), :] = (
                    ref[0, r, :, j * LANES:(j + 1) * LANES].astype(F32))
        return jnp.concatenate([il_ref[slot * halves + j] for j in range(halves)], axis=1)

    l0, l1, l2 = token_order(ls0_ref, 0), token_order(ls1_ref, 0), token_order(ls2_ref, 1)
    o0, o1, o2 = token_order(ob0_ref, 0), token_order(ob1_ref, 2), token_order(ob2_ref, 3)
    mx = jnp.maximum(jnp.maximum(l0, l1), l2)
    e0, e1, e2 = jnp.exp(l0 - mx), jnp.exp(l1 - mx), jnp.exp(l2 - mx)
    yb = (e0 * o0 + e1 * o1 + e2 * o2) / (e0 + e1 + e2)
    yb = (yb * gb_ref[0].astype(F32)).astype(BF16)
    y = (yf_ref[0].astype(F32) + yb_ref[0].astype(F32)) * zc_ref[0].astype(F32)
    yc = ((y * lax.rsqrt(jnp.mean(y * y, axis=-1, keepdims=True) + EPS)) * nw_ref[...]).astype(BF16)

    d = D_MODEL
    gates = lambda k: mg_ref[0, :, k * d:(k + 1) * d].astype(F32)
    merged = gates(0) * jnp.dot(ya, wa_ref[...], preferred_element_type=F32)
    merged = merged + gates(1) * jnp.dot(yb, wb_ref[...], preferred_element_type=F32)
    merged = merged + gates(2) * jnp.dot(yc, wc_ref[...], preferred_element_type=F32)
    out = jnp.dot(merged.astype(BF16), wo_ref[...], preferred_element_type=F32)
    o_ref[0] = x_ref[0] + gate_ref[0] * out


def _merge(x, gate, oa, ga, ob, lse, gb, yf, yb, zc, mg, ssm_w, wa, wb, wc, wo, tm=512):
    bsz, s, d = x.shape
    row = lambda w: pl.BlockSpec((1, tm, w), lambda b, i: (b, i, 0))
    phase = [pl.BlockSpec((1, dil, tm // dil, B_OUT), lambda b, i: (b, 0, i, 0)) for dil in B_DILATIONS]
    const2 = lambda shp: pl.BlockSpec(shp, lambda b, i: (0, 0))
    return pl.pallas_call(
        _merge_kernel,
        grid=(bsz, s // tm),
        in_specs=[row(d), pl.BlockSpec((1, 1, d), lambda b, i: (b, 0, 0)),
                  row(A_Q), row(A_Q)] + phase + phase +
                 [row(B_OUT), row(C_INNER), row(C_INNER), row(C_INNER),
                  row(N_BRANCH * d), const2((1, C_INNER)),
                  const2(wa.shape), const2(wb.shape), const2(wc.shape), const2(wo.shape)],
        out_specs=row(d),
        out_shape=jax.ShapeDtypeStruct((bsz, s, d), F32),
        scratch_shapes=[pltpu.VMEM((4 * B_OUT // LANES, tm, LANES), F32)],
        compiler_params=_cparams(("arbitrary", "arbitrary")),
        name="merge_out",
    )(x, gate, oa, ga, *ob, *lse, gb, yf, yb, zc, mg, ssm_w, wa, wb, wc, wo)


def _rope_tables(s):
    quarter = HEAD_DIM // 4
    freqs = ROPE_THETA ** (-jnp.arange(quarter, dtype=F32) / quarter)
    t = jnp.arange(s)
    ang_r = (t // GRID_W).astype(F32)[:, None] * freqs
    ang_c = (t % GRID_W).astype(F32)[:, None] * freqs
    cos = jnp.concatenate([jnp.cos(ang_r)] * 2 + [jnp.cos(ang_c)] * 2, axis=-1)
    sin = jnp.concatenate([-jnp.sin(ang_r), jnp.sin(ang_r), -jnp.sin(ang_c), jnp.sin(ang_c)], axis=-1)
    return jnp.tile(cos, (1, LANES // HEAD_DIM)), jnp.tile(sin, (1, LANES // HEAD_DIM))


def _layer_weights(w_in_l):
    cols = lambda k: w_in_l[:, SPLIT_OFFSETS[k]:SPLIT_OFFSETS[k + 1]]
    d = w_in_l.shape[0]
    perm = lambda w: w.reshape(d, A_HEADS, HEAD_DIM)[:, A_HEAD_ORDER, :].reshape(d, A_Q)
    dt_cols = jnp.concatenate([cols(12), cols(13), jnp.zeros((d, LANES - 2 * C_HEADS), w_in_l.dtype)], axis=1)
    pieces = {"qa": perm(cols(0)), "ka": cols(1), "ga": perm(cols(3)), "qb": cols(4), "kb": cols(5),
              "vb": cols(6), "gb": cols(7), "xbc": jnp.concatenate([cols(8), cols(10), cols(11)], axis=1),
              "zc": cols(9), "dt": dt_cols, "mg": cols(14)}
    w_main = jnp.concatenate([pieces[n] for n, _ in _W_GROUPS], axis=1).astype(BF16)
    w_t = cols(2).T.astype(BF16)
    return w_main, w_t


def _pad_row(v, width=LANES):
    return jnp.pad(v.astype(F32), (0, width - v.shape[0]))


def kernel(x, c, norm_w, w_ada, b_ada, w_in, b_gate, q_norm_a, k_norm_a, q_norm_b, k_norm_b, rel_bias,
           conv_w, conv_b, a_log, dt_bias, d_skip, ssm_norm_w, w_proj_a, w_proj_b, w_proj_c, w_out):
    bsz, s, d = x.shape
    depth = w_in.shape[0]
    c_pad = jnp.pad(c, ((0, 8 - bsz), (0, 0)))
    mod = _modulation(c_pad, w_ada, b_ada)[:, :bsz]
    cos_t, sin_t = _rope_tables(s)
    bias_tbl = _band_bias(rel_bias)
    tile2 = lambda v: jnp.tile(v.astype(F32), LANES // HEAD_DIM)
    for l in range(depth):
        shift, scale, gate = (mod[l, :, k * d:(k + 1) * d].reshape(bsz, 1, d) for k in range(3))
        w_main, w_t = _layer_weights(w_in[l])
        head_w = jnp.stack([tile2(q_norm_a[l]), tile2(k_norm_a[l]), tile2(q_norm_b[l]), tile2(k_norm_b[l])]
                           + [jnp.zeros((LANES,), F32)] * 4, axis=0)
        (qa, ka, vt, ga, *qkv_b, gb, xbc, zc, dt, mg) = _in_projection(
            x, 1.0 + scale, shift, norm_w[l].reshape(1, d), cos_t, sin_t, head_w,
            b_gate[l].reshape(1, -1), jnp.pad(conv_w[l], ((0, 8 - C_CONV), (0, 0))),
            conv_b[l].reshape(1, C_XBC), w_main, w_t)
        oa = _mixer_a(qa, ka, vt, _score_bound(q_norm_a[l], k_norm_a[l]))
        bound_b = (1.05 * HEAD_DIM ** 0.5 * jnp.max(jnp.abs(q_norm_b[l])) * jnp.max(jnp.abs(k_norm_b[l]))
                   + jnp.max(jnp.abs(rel_bias)))
        ob, lse = zip(*[_mixer_b_group(qkv_b[g], qkv_b[B_GROUPS + g], qkv_b[2 * B_GROUPS + g], bias_tbl[g],
                                       bound_b, g) for g in range(B_GROUPS)])
        prm = jnp.stack([_pad_row(dt_bias[l].reshape(-1)), _pad_row(a_log[l].reshape(-1)),
                         _pad_row(jnp.tile(d_skip[l], 2))] + [jnp.zeros((LANES,), F32)] * 5, axis=0)
        yf, yb = _ssd(xbc, dt, prm)
        perm_rows = lambda w: w.reshape(A_HEADS, HEAD_DIM, d)[A_HEAD_ORDER, :, :].reshape(A_Q, d)
        x = _merge(x, gate, oa, ga, ob, lse, gb, yf, yb, zc, mg, ssm_norm_w[l].reshape(1, -1),
                   perm_rows(w_proj_a[l]).astype(BF16), w_proj_b[l].astype(BF16),
                   w_proj_c[l].astype(BF16), w_out[l].astype(BF16))
    return x
```

```python
import functools
import math

import numpy as np
import jax
import jax.numpy as jnp
from jax import lax
from jax.experimental import pallas as pl
from jax.experimental.pallas import tpu as pltpu

F32 = jnp.float32
BF16 = jnp.bfloat16

D_MODEL = 1024
HEAD_DIM = 64
EPS = 1e-6
NEG_INF = -1e30
GRID_W = 64
A_HEADS = 8
A_KV_HEADS = 2
ROPE_THETA = 10000.0
B_GROUPS = 3
B_HEADS_PER_GROUP = 4
B_WINDOWS = (128, 512, 2048)
B_DILATIONS = (1, 4, 16)
N_BUCKETS = 32
MAX_DISTANCE = 1024
C_HEADS = 8
C_HEAD_DIM = 64
C_INNER = C_HEADS * C_HEAD_DIM
C_GROUPS = 2
C_STATE = 128
C_CONV = 5
C_CHUNK = 128
N_BRANCH = 3

A_Q = A_HEADS * HEAD_DIM
A_KV = A_KV_HEADS * HEAD_DIM
B_QKV = B_GROUPS * B_HEADS_PER_GROUP * HEAD_DIM
B_OUT = B_HEADS_PER_GROUP * HEAD_DIM
C_BC = C_GROUPS * C_STATE
C_XBC = C_INNER + 2 * C_BC
SPLIT_WIDTHS = (A_Q, A_KV, A_KV, A_Q, B_QKV, B_QKV, B_QKV, B_OUT,
                C_INNER, C_INNER, C_BC, C_BC, C_HEADS, C_HEADS, N_BRANCH * D_MODEL)
SPLIT_OFFSETS = tuple(int(v) for v in np.cumsum((0,) + SPLIT_WIDTHS))

LANES = 128
VMEM_LIMIT = 56 * 1024 * 1024
LOG2E = math.log2(math.e)

_W_GROUPS = (("qa", A_Q), ("ka", A_KV), ("dt", LANES), ("ga", A_Q), ("qb", B_QKV), ("kb", B_QKV), ("vb", B_QKV),
             ("gb", B_OUT), ("xbc", C_XBC), ("zc", C_INNER), ("mg", N_BRANCH * D_MODEL))
MXU_COLS = 256
_W_OFF = {}
_o = 0
for _n, _w in _W_GROUPS:
    _W_OFF[_n] = (_o, _w)
    _o += _w
W_MAIN_COLS = _o

A_HEAD_ORDER = tuple(h for j in range(4) for h in (j, j + 4))


def _cparams(sem):
    return pltpu.CompilerParams(dimension_semantics=sem, vmem_limit_bytes=VMEM_LIMIT)


def _silu(x):
    return x * jax.nn.sigmoid(x)


def _mod_kernel(c_ref, w_ref, b_ref, o_ref):
    c = c_ref[...]
    o_ref[0] = jnp.dot(_silu(c), w_ref[0], preferred_element_type=F32,
                       precision=lax.Precision.HIGHEST) + b_ref[0]


def _modulation(c_pad, w_ada, b_ada):
    depth, d, n = w_ada.shape
    tn = 512
    return pl.pallas_call(
        _mod_kernel,
        grid=(depth, n // tn),
        in_specs=[pl.BlockSpec((c_pad.shape[0], d), lambda l, j: (0, 0)),
                  pl.BlockSpec((1, d, tn), lambda l, j: (l, 0, j)),
                  pl.BlockSpec((1, 1, tn), lambda l, j: (l, 0, j))],
        out_specs=pl.BlockSpec((1, c_pad.shape[0], tn), lambda l, j: (l, 0, j)),
        out_shape=jax.ShapeDtypeStruct((depth, c_pad.shape[0], n), F32),
        compiler_params=_cparams(("arbitrary", "arbitrary")),
        name="adaln_mod",
    )(c_pad, w_ada, b_ada.reshape(depth, 1, n))


def _head_sumsq_matrix():
    r = lax.broadcasted_iota(jnp.int32, (MXU_COLS, MXU_COLS), 0) // HEAD_DIM
    c = lax.broadcasted_iota(jnp.int32, (MXU_COLS, MXU_COLS), 1) // HEAD_DIM
    return jnp.where(r == c, 1.0 / HEAD_DIM, 0.0).astype(BF16)


def _inproj_kernel(x_ref, xp_ref, xn_ref, sc_ref, sh_ref, nw_ref, cos_ref, sin_ref, hw_ref, bg_ref,
                   cw_ref, cb_ref, w_ref, wt_ref,
                   qa_ref, ka_ref, vt_ref, ga_ref,
                   qb0_ref, qb1_ref, qb2_ref, kb0_ref, kb1_ref, kb2_ref, vb0_ref, vb1_ref, vb2_ref,
                   gb_ref, xbc_ref, zc_ref, dt_ref, mg_ref, h_ref, ext_ref):
    tm = x_ref.shape[1]

    def modulated(xv):
        ms = jnp.mean(xv * xv, axis=-1, keepdims=True)
        return ((xv * lax.rsqrt(ms + EPS)) * nw_ref[...]) * sc_ref[0] + sh_ref[0]

    h = modulated(x_ref[0])
    hb = h.astype(BF16)
    for j in range(D_MODEL // LANES):
        h_ref[j] = h[:, j * LANES:(j + 1) * LANES]

    bd = _head_sumsq_matrix()
    lane = lax.broadcasted_iota(jnp.int32, (1, LANES), 1)
    first_half = (lane % 32) < 16
    cos = cos_ref[...]
    sin = sin_ref[...]

    def proj(name, c0, c1):
        off, _ = _W_OFF[name]
        return jnp.dot(hb, w_ref[:, off + c0:off + c1], preferred_element_type=F32)

    def head_norm(y, w_row):
        n = y.shape[1]
        ss = jnp.dot((y * y).astype(BF16), bd[:n, :n], preferred_element_type=F32)
        return (y * lax.rsqrt(ss + EPS)) * jnp.concatenate([w_row] * (n // LANES), axis=1)

    def rope(y):
        sw = jnp.where(first_half, pltpu.roll(y, LANES - 16, 1), pltpu.roll(y, 16, 1))
        return y * cos + sw * sin

    deferred = []

    def normed(y, finish):
        waiting = list(deferred)
        deferred[:] = [(y, finish)]
        for py, pfinish in waiting:
            pfinish(py)

    def finish_qa(y, c0):
        y = head_norm(y, hw_ref[0:1, :])
        for j in range(0, y.shape[1], LANES):
            blk = rope(y[:, j:j + LANES]) * (HEAD_DIM ** -0.5 * LOG2E)
            qa_ref[0, :, c0 + j:c0 + j + LANES] = blk.astype(BF16)

    def finish_ka(y):
        ka_ref[0] = rope(head_norm(y, hw_ref[1:2, :])).astype(BF16)

    for c0 in range(0, A_Q, MXU_COLS):
        normed(proj("qa", c0, c0 + MXU_COLS), functools.partial(finish_qa, c0=c0))
    ka_dt = proj("ka", 0, A_KV + LANES)
    dt_ref[0] = ka_dt[:, A_KV:]
    normed(ka_dt[:, :A_KV], finish_ka)
    vt_ref[0] = lax.dot_general(wt_ref[...], hb, (((1,), (1,)), ((), ())),
                                preferred_element_type=F32).astype(BF16)
    for j in range(0, A_Q, 256):
        ga_ref[0, :, j:j + 256] = _silu(proj("ga", j, j + 256)).astype(BF16)
    def store_phases(ref, lanes, y, d):
        n = tm // d
        for r in range(d):
            ref[0, r, :, lanes] = y[r * n:(r + 1) * n, :]

    for g, (qr, kr, vr) in enumerate(((qb0_ref, kb0_ref, vb0_ref), (qb1_ref, kb1_ref, vb1_ref),
                                      (qb2_ref, kb2_ref, vb2_ref))):
        d = B_DILATIONS[g]
        if d == 1:
            lhs = hb
        else:
            lhs = jnp.concatenate(
                [jnp.concatenate([h_ref[j, pl.ds(r, tm // d, stride=d), :] for r in range(d)], axis=0)
                 for j in range(D_MODEL // LANES)], axis=1).astype(BF16)

        def proj_g(name, c0, c1, lhs=lhs):
            off, _ = _W_OFF[name]
            return jnp.dot(lhs, w_ref[:, off + c0:off + c1], preferred_element_type=F32)

        def finish_q(y, ref=qr, d=d):
            store_phases(ref, slice(0, B_OUT), (head_norm(y, hw_ref[2:3, :]) * (HEAD_DIM ** -0.5)).astype(BF16), d)

        def finish_k(y, ref=kr, d=d):
            store_phases(ref, slice(0, B_OUT), head_norm(y, hw_ref[3:4, :]).astype(BF16), d)

        normed(proj_g("qb", g * B_OUT, (g + 1) * B_OUT), finish_q)
        normed(proj_g("kb", g * B_OUT, (g + 1) * B_OUT), finish_k)
        v = proj_g("vb", g * B_OUT, (g + 1) * B_OUT)
        store_phases(vr, slice(0, B_OUT), v.astype(BF16), d)
    normed(None, None)
    deferred.clear()
    gb_ref[0] = _silu(proj("gb", 0, B_OUT)).astype(BF16)
    i = pl.program_id(1)
    n_t = pl.num_programs(1)
    halo = xp_ref.shape[1]
    ext_lhs = jnp.concatenate([modulated(xp_ref[0]).astype(BF16), hb, modulated(xn_ref[0]).astype(BF16)], axis=0)
    row = lax.broadcasted_iota(jnp.int32, (tm + 2 * halo, 1), 0)
    in_seq = jnp.logical_and(jnp.logical_or(row >= halo, i > 0),
                             jnp.logical_or(row < halo + tm, i < n_t - 1))
    off_xbc, _ = _W_OFF["xbc"]
    for j in range(0, C_XBC, 256):
        cols = slice(j, j + 256)
        ext = jnp.dot(ext_lhs, w_ref[:, off_xbc + j:off_xbc + j + 256], preferred_element_type=F32)
        ext_ref[:, cols] = jnp.where(in_seq, ext, 0.0)
        acc = jnp.zeros((tm, 256), F32) + cb_ref[:, cols]
        for k in range(C_CONV):
            start = halo + k - C_CONV // 2
            acc = acc + ext_ref[start:start + tm, cols] * cw_ref[k:k + 1, cols]
        xbc_ref[0, :, cols] = _silu(acc).astype(BF16)
    for j in range(0, C_INNER, 256):
        zc_ref[0, :, j:j + 256] = _silu(proj("zc", j, j + 256)).astype(BF16)
    for j in range(0, N_BRANCH * D_MODEL, 512):
        mg_ref[0, :, j:j + 512] = jax.nn.sigmoid(proj("mg", j, j + 512) + bg_ref[:, j:j + 512]).astype(BF16)


def _in_projection(x, sc1p, shift, norm_w, cos_t, sin_t, head_w, b_gate, conv_w, conv_b, w_main, w_t, tm=256):
    bsz, s, d = x.shape
    nt = s // tm
    hb = tm // CONV_HALO
    n_h = s // CONV_HALO
    row = lambda i, w: pl.BlockSpec((1, tm, w), lambda b, i: (b, i, 0))
    const2 = lambda shp: pl.BlockSpec(shp, lambda b, i: (0, 0))
    phase = [("phase", dil) for _ in range(3) for dil in B_DILATIONS]
    outs = ([("qa", A_Q, BF16), ("ka", A_KV, BF16), None, ("ga", A_Q, BF16)] + phase +
            [("gb", B_OUT, BF16), ("xbc", C_XBC, BF16),
             ("zc", C_INNER, BF16), ("dt", LANES, F32), ("mg", N_BRANCH * D_MODEL, BF16)])
    out_shape, out_specs = [], []
    for o in outs:
        if o is None:
            out_shape.append(jax.ShapeDtypeStruct((bsz, A_KV, s), BF16))
            out_specs.append(pl.BlockSpec((1, A_KV, tm), lambda b, i: (b, 0, i)))
        elif o[0] == "phase":
            out_shape.append(jax.ShapeDtypeStruct((bsz, o[1], s // o[1], B_OUT), BF16))
            out_specs.append(pl.BlockSpec((1, o[1], tm // o[1], B_OUT), lambda b, i: (b, 0, i, 0)))
        else:
            out_shape.append(jax.ShapeDtypeStruct((bsz, s, o[1]), o[2]))
            out_specs.append(row(0, o[1]))
    return pl.pallas_call(
        _inproj_kernel,
        grid=(bsz, nt),
        in_specs=[row(0, d),
                  pl.BlockSpec((1, CONV_HALO, d), lambda b, i: (b, jnp.maximum(i * hb - 1, 0), 0)),
                  pl.BlockSpec((1, CONV_HALO, d), lambda b, i: (b, jnp.minimum((i + 1) * hb, n_h - 1), 0)),
                  pl.BlockSpec((1, 1, d), lambda b, i: (b, 0, 0)),
                  pl.BlockSpec((1, 1, d), lambda b, i: (b, 0, 0)),
                  const2((1, d)),
                  pl.BlockSpec((tm, LANES), lambda b, i: (i, 0)),
                  pl.BlockSpec((tm, LANES), lambda b, i: (i, 0)),
                  const2((8, LANES)),
                  const2(b_gate.shape),
                  const2(conv_w.shape),
                  const2(conv_b.shape),
                  const2(w_main.shape),
                  const2(w_t.shape)],
        out_specs=out_specs,
        out_shape=out_shape,
        scratch_shapes=[pltpu.VMEM((d // LANES, tm, LANES), F32),
                        pltpu.VMEM((tm + 2 * CONV_HALO, C_XBC), F32)],
        compiler_params=_cparams(("arbitrary", "arbitrary")),
        name="norm_inproj",
    )(x, x, x, sc1p, shift, norm_w, cos_t, sin_t, head_w, b_gate, conv_w, conv_b, w_main, w_t)


A_AUG = HEAD_DIM + 8


def _attn_a_kernel(bounded_ref, q_ref, k_ref, vt_ref, o_ref, qt_ref, m_ref, acc_ref, st_ref, *, kc):
    tq = q_ref.shape[1]
    s_len = k_ref.shape[1]
    n_blk = A_Q // LANES
    lane = lax.broadcasted_iota(jnp.int32, (tq, LANES), 1)
    for j in range(n_blk):
        qblk = q_ref[0, :, j * LANES:(j + 1) * LANES]
        for kv in range(A_KV_HEADS):
            in_half = (lane < HEAD_DIM) if kv == 0 else (lane >= HEAD_DIM)
            qt_ref[A_KV_HEADS * j + kv] = jnp.where(in_half, qblk, jnp.zeros_like(qblk))
    m_ref[...] = jnp.full(m_ref.shape, NEG_INF, F32)
    acc_ref[...] = jnp.zeros(acc_ref.shape, F32)
    ones = jnp.ones((A_AUG - HEAD_DIM, kc), BF16)

    n_chunks = s_len // kc

    def scores(c, h):
        off = pl.multiple_of(c * kc, kc)
        return lax.dot_general(k_ref[0, pl.ds(off, kc), :], qt_ref[h], (((1,), (1,)), ((), ())),
                               preferred_element_type=F32)

    ahead = st_ref.shape[0]

    def sweep(online, unroll):
        for i in range(ahead):
            st_ref[i] = scores(0, i)

        def vaug(c, kv):
            off = pl.multiple_of(c * kc, kc)
            return jnp.concatenate([vt_ref[0, kv * HEAD_DIM:(kv + 1) * HEAD_DIM, pl.ds(off, kc)], ones], axis=0)

        def body(cc, carry):
            pending = [st_ref[i] for i in range(ahead)]
            for t in range(unroll * A_HEADS):
                c, h = cc * unroll + t // A_HEADS, t % A_HEADS
                nxt = t + ahead
                c_nxt = jnp.minimum(cc * unroll + nxt // A_HEADS, n_chunks - 1)
                pending.append(scores(c_nxt, nxt % A_HEADS))
                st = pending.pop(0)
                v = vaug(c, h % A_KV_HEADS)
                if online:
                    m = m_ref[h:h + 1, :]
                    m_new = jnp.maximum(m, jnp.max(st, axis=0, keepdims=True))
                    alpha = jnp.exp2(m - m_new)
                    p = jnp.exp2(st - m_new).astype(BF16)
                    acc_ref[h] = alpha * acc_ref[h] + jnp.dot(v, p, preferred_element_type=F32)
                    m_ref[h:h + 1, :] = m_new
                else:
                    p = jnp.exp2(st).astype(BF16)
                    acc_ref[h] += jnp.dot(v, p, preferred_element_type=F32)
            for i in range(ahead):
                st_ref[i] = pending[i]
            return carry

        lax.fori_loop(0, n_chunks // unroll, body, 0)

    bounded = bounded_ref[0] == 1
    pl.when(bounded)(lambda: sweep(False, 16))
    pl.when(jnp.logical_not(bounded))(lambda: sweep(True, 1))

    for j in range(n_blk):
        halves = []
        for kv in range(A_KV_HEADS):
            a = acc_ref[A_KV_HEADS * j + kv]
            halves.append(a[:HEAD_DIM] / a[HEAD_DIM:HEAD_DIM + 1])
        o_ref[0, :, j * LANES:(j + 1) * LANES] = jnp.concatenate(halves, axis=0).T.astype(o_ref.dtype)


A_SCORE_LIMIT = 40.0


def _score_bound(q_norm, k_norm):
    return (1.05 * HEAD_DIM * HEAD_DIM ** -0.5 * LOG2E) * jnp.max(jnp.abs(q_norm)) * jnp.max(jnp.abs(k_norm))


def _mixer_a(qa, ka, vt, bound, tq=256, kc=512):
    bsz, s, _ = qa.shape
    bounded = (bound <= A_SCORE_LIMIT).astype(jnp.int32).reshape(1)
    return pl.pallas_call(
        functools.partial(_attn_a_kernel, kc=kc),
        grid=(bsz, s // tq),
        in_specs=[pl.BlockSpec(memory_space=pltpu.SMEM),
                  pl.BlockSpec((1, tq, A_Q), lambda b, i: (b, i, 0)),
                  pl.BlockSpec((1, s, A_KV), lambda b, i: (b, 0, 0)),
                  pl.BlockSpec((1, A_KV, s), lambda b, i: (b, 0, 0))],
        out_specs=pl.BlockSpec((1, tq, A_Q), lambda b, i: (b, i, 0)),
        out_shape=jax.ShapeDtypeStruct((bsz, s, A_Q), BF16),
        scratch_shapes=[pltpu.VMEM((A_HEADS, tq, LANES), BF16),
                        pltpu.VMEM((A_HEADS, tq), F32),
                        pltpu.VMEM((A_HEADS, A_AUG, tq), F32),
                        pltpu.VMEM((2, kc, tq), F32)],
        compiler_params=_cparams(("arbitrary", "arbitrary")),
        name="mixer_a_attn",
    )(bounded, qa, ka, vt)


B_SIDE = 64
B_TQ = 256
B_BAND = B_TQ + 2 * B_SIDE
B_STEP = 4 * B_TQ


def _attn_b_kernel(bounded_ref, q_ref, kp_ref, kc_ref, kn_ref, vp_ref, vc_ref, vn_ref, bias_ref, o_ref, lse_ref):
    bounded = bounded_ref[0] == 1
    pl.when(bounded)(functools.partial(
        _attn_b_body, q_ref, kp_ref, kc_ref, kn_ref, vp_ref, vc_ref, vn_ref, bias_ref, o_ref, lse_ref, False))
    pl.when(jnp.logical_not(bounded))(functools.partial(
        _attn_b_body, q_ref, kp_ref, kc_ref, kn_ref, vp_ref, vc_ref, vn_ref, bias_ref, o_ref, lse_ref, True))


def _attn_b_body(q_ref, kp_ref, kc_ref, kn_ref, vp_ref, vc_ref, vn_ref, bias_ref, o_ref, lse_ref, use_max):
    mi = pl.program_id(2)
    n_m = pl.num_programs(2)
    n_sub = q_ref.shape[2] // B_TQ
    kall = jnp.concatenate([kp_ref[0, 0], kc_ref[0, 0], kn_ref[0, 0]], axis=0)
    vall = jnp.concatenate([vp_ref[0, 0], vc_ref[0, 0], vn_ref[0, 0]], axis=0)
    kk = lax.broadcasted_iota(jnp.int32, (1, B_BAND), 1)
    lane = lax.broadcasted_iota(jnp.int32, (B_TQ, B_OUT), 1) // HEAD_DIM
    for t in range(n_sub):
        q = q_ref[0, 0, t * B_TQ:(t + 1) * B_TQ, :]
        kband = kall[t * B_TQ:t * B_TQ + B_BAND]
        vband = vall[t * B_TQ:t * B_TQ + B_BAND]
        key_ok = None
        if t == 0:
            key_ok = jnp.logical_or(kk >= B_SIDE, mi > 0)
        if t == n_sub - 1:
            hi_ok = jnp.logical_or(kk < B_SIDE + B_TQ, mi < n_m - 1)
            key_ok = hi_ok if key_ok is None else jnp.logical_and(key_ok, hi_ok)
        o_acc = jnp.zeros((B_TQ, B_OUT), F32)
        lse_acc = jnp.zeros((B_TQ, B_OUT), F32)
        scores = [lax.dot_general(jnp.where(lane == h, q, jnp.zeros_like(q)), kband, (((1,), (1,)), ((), ())),
                                  preferred_element_type=F32) for h in range(B_HEADS_PER_GROUP)]
        for h in range(B_HEADS_PER_GROUP):
            s = scores[h] + bias_ref[h]
            if key_ok is not None:
                s = jnp.where(key_ok, s, NEG_INF)
            if use_max:
                mx = jnp.max(s, axis=-1, keepdims=True)
                e = jnp.exp(s - mx)
            else:
                e = jnp.exp(s)
            den = jnp.sum(e, axis=-1, keepdims=True)
            p = (e / den).astype(BF16)
            oh = jnp.dot(p, vband, preferred_element_type=F32)
            o_acc = jnp.where(lane == h, oh, o_acc)
            lse = mx + jnp.log(den) if use_max else jnp.log(den)
            lse_acc = jnp.where(lane == h, lse, lse_acc)
        o_ref[0, 0, t * B_TQ:(t + 1) * B_TQ, :] = o_acc.astype(o_ref.dtype)
        lse_ref[0, 0, t * B_TQ:(t + 1) * B_TQ, :] = lse_acc


B_SCORE_LIMIT = 60.0


def _mixer_b_group(qb, kb, vb, bias_g, bound, g):
    bsz, d, m, _ = qb.shape
    bounded = (bound <= B_SCORE_LIMIT).astype(jnp.int32).reshape(1)
    step = min(B_STEP, m)
    assert m % step == 0 and step % B_TQ == 0, (m, step)
    n_m = m // step
    hb = step // B_SIDE
    n_side = m // B_SIDE
    cur = pl.BlockSpec((1, 1, step, B_OUT), lambda b, r, i: (b, r, i, 0))
    prev = pl.BlockSpec((1, 1, B_SIDE, B_OUT), lambda b, r, i: (b, r, jnp.maximum(i * hb - 1, 0), 0))
    nxt = pl.BlockSpec((1, 1, B_SIDE, B_OUT),
                       lambda b, r, i: (b, r, jnp.minimum((i + 1) * hb, n_side - 1), 0))
    return pl.pallas_call(
        _attn_b_kernel,
        grid=(bsz, d, n_m),
        in_specs=[pl.BlockSpec(memory_space=pltpu.SMEM), cur, prev, cur, nxt, prev, cur, nxt,
                  pl.BlockSpec((B_HEADS_PER_GROUP, B_TQ, B_BAND), lambda b, r, i: (0, 0, 0))],
        out_specs=[cur, cur],
        out_shape=[jax.ShapeDtypeStruct((bsz, d, m, B_OUT), BF16),
                   jax.ShapeDtypeStruct((bsz, d, m, B_OUT), F32)],
        compiler_params=_cparams(("arbitrary", "arbitrary", "arbitrary")),
        name=f"mixer_b_attn_g{g}",
    )(bounded, qb, kb, kb, kb, vb, vb, vb, bias_g)


def _t5_bucket(rel):
    nb = N_BUCKETS // 2
    max_exact = nb // 2
    ret = jnp.where(rel > 0, nb, 0)
    n = jnp.abs(rel)
    nf = jnp.maximum(n, 1).astype(F32)
    large = max_exact + (jnp.log(nf / max_exact) / math.log(MAX_DISTANCE / max_exact)
                         * (nb - max_exact)).astype(jnp.int32)
    large = jnp.minimum(large, nb - 1)
    return ret + jnp.where(n < max_exact, n, large)


def _band_bias(rel_bias):
    rel = jnp.arange(-B_SIDE, B_SIDE + 1)
    n_rel = 2 * B_SIDE + 1
    width = B_TQ + B_BAND
    nh = B_HEADS_PER_GROUP
    tables = []
    for g in range(B_GROUPS):
        tbl = rel_bias[:, g * nh:(g + 1) * nh]
        b1 = tbl[_t5_bucket(rel * B_DILATIONS[g])].T.astype(F32)
        w = jnp.concatenate([b1, jnp.full((nh, width - n_rel), NEG_INF, F32)], axis=1)
        flat = jnp.broadcast_to(w[:, None, :], (nh, B_TQ, width)).reshape(nh, B_TQ * width)
        tables.append(flat[:, :B_TQ * (width - 1)].reshape(nh, B_TQ, width - 1)[:, :, :B_BAND])
    return jnp.stack(tables, axis=0)


CONV_HALO = 8


def _split3(x):
    hi = x.astype(BF16)
    r1 = x - hi.astype(F32)
    mid = r1.astype(BF16)
    lo = (r1 - mid.astype(F32)).astype(BF16)
    return hi, mid, lo


def _ssd_direction(x_ref, dt_ref, prm_ref, state_ref, y_ref, *, reverse):
    L = C_CHUNK
    col0 = C_HEADS if reverse else 0
    r = lax.broadcasted_iota(jnp.int32, (L, L), 0)
    c = lax.broadcasted_iota(jnp.int32, (L, L), 1)
    causal = (r <= c) if reverse else (r >= c)
    tri = jnp.where(causal, 1.0, 0.0).astype(BF16)

    groups = []
    for g in range(C_GROUPS):
        lanes = slice(g * 256, (g + 1) * 256)
        xg = x_ref[0, :, lanes].astype(F32)
        bm = x_ref[0, :, C_INNER + g * C_STATE:C_INNER + (g + 1) * C_STATE]
        cm = x_ref[0, :, C_INNER + C_BC + g * C_STATE:C_INNER + C_BC + (g + 1) * C_STATE]
        cb = lax.dot_general(cm, bm, (((1,), (1,)), ((), ())), preferred_element_type=F32)
        state = state_ref[g]
        y_in = jnp.dot(cm, state.astype(BF16), preferred_element_type=F32)
        groups.append((lanes, xg, bm, cb, state, y_in))
    yield

    dt_bias = prm_ref[0:1, :]
    a_neg = -jnp.exp(prm_ref[1:2, :])
    dtv = jax.nn.softplus(dt_ref[0] + dt_bias)
    a = dtv * a_neg
    hi, mid, lo = _split3(a)
    cs2 = jnp.dot(tri, jnp.concatenate([hi, mid], axis=1), preferred_element_type=F32)
    cs = cs2[:, :LANES] + cs2[:, LANES:] + jnp.dot(tri, lo, preferred_element_type=F32)
    yield
    cs_t = cs.T
    tot = cs[0:1, :] if reverse else cs[L - 1:L, :]
    decay_in = jnp.exp(cs)
    decay_out = jnp.exp(tot - cs)
    chunk_decay = jnp.exp(tot)

    lane_head = lax.broadcasted_iota(jnp.int32, (1, 4 * C_HEAD_DIM), 1) // C_HEAD_DIM

    def expand_row(m, g):
        out = jnp.zeros((1, 4 * C_HEAD_DIM), F32)
        for j in range(4):
            cj = col0 + g * 4 + j
            out = jnp.where(lane_head == j, m[:, cj:cj + 1], out)
        return out

    sel = (lax.broadcasted_iota(jnp.int32, (LANES, C_INNER), 0)
           == col0 + lax.broadcasted_iota(jnp.int32, (LANES, C_INNER), 1) // C_HEAD_DIM)
    sel = jnp.where(sel, 1.0, 0.0).astype(BF16)

    sel2 = jnp.concatenate([sel, sel], axis=0)

    def expand(m):
        hi = m.astype(BF16)
        lo = (m - hi.astype(F32)).astype(BF16)
        return jnp.dot(jnp.concatenate([hi, lo], axis=1), sel2, preferred_element_type=F32)

    dt_e, din_e, dout_e = expand(dtv), expand(decay_in), expand(decay_out)
    yield

    for g, (lanes, xg, bm, cb, state, y_in) in enumerate(groups):
        xdt = xg * dt_e[:, lanes]
        y = y_in * din_e[:, lanes]
        xdt_b = xdt.astype(BF16)
        parts = []
        for j in range(4):
            cj = col0 + g * 4 + j
            diff = cs[:, cj:cj + 1] - cs_t[cj:cj + 1, :]
            lm = jnp.exp(jnp.where(causal, diff, NEG_INF))
            gmat = (cb * lm).astype(BF16)
            parts.append(jnp.dot(gmat, xdt_b[:, j * C_HEAD_DIM:(j + 1) * C_HEAD_DIM],
                                 preferred_element_type=F32))
        y = y + jnp.concatenate(parts, axis=1)
        if not reverse:
            y = y + xg * expand_row(prm_ref[2:3, :], g)
        y_ref[0, :, lanes] = y.astype(y_ref.dtype)
        xs = (xdt * dout_e[:, lanes]).astype(BF16)
        upd = lax.dot_general(bm, xs, (((0,), (0,)), ((), ())), preferred_element_type=F32)
        state_ref[g] = state * expand_row(chunk_decay, g) + upd
    yield


def _ssd_kernel(xf_ref, xb_ref, dtf_ref, dtb_ref, prm_ref, yf_ref, yb_ref, sf_ref, sb_ref):
    @pl.when(pl.program_id(1) == 0)
    def _():
        sf_ref[...] = jnp.zeros_like(sf_ref)
        sb_ref[...] = jnp.zeros_like(sb_ref)

    forward = _ssd_direction(xf_ref, dtf_ref, prm_ref, sf_ref, yf_ref, reverse=False)
    backward = _ssd_direction(xb_ref, dtb_ref, prm_ref, sb_ref, yb_ref, reverse=True)
    for _ in zip(forward, backward):
        pass


def _ssd(xact, dt, prm):
    bsz, s, ch = xact.shape
    n_c = s // C_CHUNK
    fwd = lambda w: pl.BlockSpec((1, C_CHUNK, w), lambda b, i: (b, i, 0))
    bwd = lambda w: pl.BlockSpec((1, C_CHUNK, w), lambda b, i: (b, n_c - 1 - i, 0))
    return pl.pallas_call(
        _ssd_kernel,
        grid=(bsz, n_c),
        in_specs=[fwd(ch), bwd(ch), fwd(LANES), bwd(LANES), pl.BlockSpec((8, LANES), lambda b, i: (0, 0))],
        out_specs=[fwd(C_INNER), bwd(C_INNER)],
        out_shape=[jax.ShapeDtypeStruct((bsz, s, C_INNER), BF16)] * 2,
        scratch_shapes=[pltpu.VMEM((C_GROUPS, C_STATE, 4 * C_HEAD_DIM), F32)] * 2,
        compiler_params=_cparams(("arbitrary", "arbitrary")),
        name="mixer_c_ssd",
    )(xact, xact, dt, dt, prm)


def _merge_kernel(x_ref, gate_ref, oa_ref, ga_ref, ob0_ref, ob1_ref, ob2_ref, ls0_ref, ls1_ref, ls2_ref,
                  gb_ref, yf_ref, yb_ref, zc_ref, mg_ref, nw_ref, wa_ref, wb_ref, wc_ref, wo_ref,
                  o_ref, il_ref):
    ya = (oa_ref[0].astype(F32) * ga_ref[0].astype(F32)).astype(BF16)

    def token_order(ref, slot):
        d, n = ref.shape[1], ref.shape[2]
        if d == 1:
            return ref[0, 0].astype(F32)
        halves = B_OUT // LANES
        for r in range(d):
            for j in range(halves):
                il_ref[slot * halves + j, pl.ds(r, n, stride=d), :] = (
                    ref[0, r, :, j * LANES:(j + 1) * LANES].astype(F32))
        return jnp.concatenate([il_ref[slot * halves + j] for j in range(halves)], axis=1)

    l0, l1, l2 = token_order(ls0_ref, 0), token_order(ls1_ref, 0), token_order(ls2_ref, 1)
    o0, o1, o2 = token_order(ob0_ref, 0), token_order(ob1_ref, 2), token_order(ob2_ref, 3)
    mx = jnp.maximum(jnp.maximum(l0, l1), l2)
    e0, e1, e2 = jnp.exp(l0 - mx), jnp.exp(l1 - mx), jnp.exp(l2 - mx)
    yb = (e0 * o0 + e1 * o1 + e2 * o2) / (e0 + e1 + e2)
    yb = (yb * gb_ref[0].astype(F32)).astype(BF16)
    y = (yf_ref[0].astype(F32) + yb_ref[0].astype(F32)) * zc_ref[0].astype(F32)
    yc = ((y * lax.rsqrt(jnp.mean(y * y, axis=-1, keepdims=True) + EPS)) * nw_ref[...]).astype(BF16)

    d = D_MODEL
    gates = lambda k: mg_ref[0, :, k * d:(k + 1) * d].astype(F32)
    merged = gates(0) * jnp.dot(ya, wa_ref[...], preferred_element_type=F32)
    merged = merged + gates(1) * jnp.dot(yb, wb_ref[...], preferred_element_type=F32)
    merged = merged + gates(2) * jnp.dot(yc, wc_ref[...], preferred_element_type=F32)
    out = jnp.dot(merged.astype(BF16), wo_ref[...], preferred_element_type=F32)
    o_ref[0] = x_ref[0] + gate_ref[0] * out


def _merge(x, gate, oa, ga, ob, lse, gb, yf, yb, zc, mg, ssm_w, wa, wb, wc, wo, tm=512):
    bsz, s, d = x.shape
    row = lambda w: pl.BlockSpec((1, tm, w), lambda b, i: (b, i, 0))
    phase = [pl.BlockSpec((1, dil, tm // dil, B_OUT), lambda b, i: (b, 0, i, 0)) for dil in B_DILATIONS]
    const2 = lambda shp: pl.BlockSpec(shp, lambda b, i: (0, 0))
    return pl.pallas_call(
        _merge_kernel,
        grid=(bsz, s // tm),
        in_specs=[row(d), pl.BlockSpec((1, 1, d), lambda b, i: (b, 0, 0)),
                  row(A_Q), row(A_Q)] + phase + phase +
                 [row(B_OUT), row(C_INNER), row(C_INNER), row(C_INNER),
                  row(N_BRANCH * d), const2((1, C_INNER)),
                  const2(wa.shape), const2(wb.shape), const2(wc.shape), const2(wo.shape)],
        out_specs=row(d),
        out_shape=jax.ShapeDtypeStruct((bsz, s, d), F32),
        scratch_shapes=[pltpu.VMEM((4 * B_OUT // LANES, tm, LANES), F32)],
        compiler_params=_cparams(("arbitrary", "arbitrary")),
        name="merge_out",
    )(x, gate, oa, ga, *ob, *lse, gb, yf, yb, zc, mg, ssm_w, wa, wb, wc, wo)


def _rope_tables(s):
    quarter = HEAD_DIM // 4
    freqs = ROPE_THETA ** (-jnp.arange(quarter, dtype=F32) / quarter)
    t = jnp.arange(s)
    ang_r = (t // GRID_W).astype(F32)[:, None] * freqs
    ang_c = (t % GRID_W).astype(F32)[:, None] * freqs
    cos = jnp.concatenate([jnp.cos(ang_r)] * 2 + [jnp.cos(ang_c)] * 2, axis=-1)
    sin = jnp.concatenate([-jnp.sin(ang_r), jnp.sin(ang_r), -jnp.sin(ang_c), jnp.sin(ang_c)], axis=-1)
    return jnp.tile(cos, (1, LANES // HEAD_DIM)), jnp.tile(sin, (1, LANES // HEAD_DIM))


def _layer_weights(w_in_l):
    cols = lambda k: w_in_l[:, SPLIT_OFFSETS[k]:SPLIT_OFFSETS[k + 1]]
    d = w_in_l.shape[0]
    perm = lambda w: w.reshape(d, A_HEADS, HEAD_DIM)[:, A_HEAD_ORDER, :].reshape(d, A_Q)
    dt_cols = jnp.concatenate([cols(12), cols(13), jnp.zeros((d, LANES - 2 * C_HEADS), w_in_l.dtype)], axis=1)
    pieces = {"qa": perm(cols(0)), "ka": cols(1), "ga": perm(cols(3)), "qb": cols(4), "kb": cols(5),
              "vb": cols(6), "gb": cols(7), "xbc": jnp.concatenate([cols(8), cols(10), cols(11)], axis=1),
              "zc": cols(9), "dt": dt_cols, "mg": cols(14)}
    w_main = jnp.concatenate([pieces[n] for n, _ in _W_GROUPS], axis=1).astype(BF16)
    w_t = cols(2).T.astype(BF16)
    return w_main, w_t


def _pad_row(v, width=LANES):
    return jnp.pad(v.astype(F32), (0, width - v.shape[0]))


def kernel(x, c, norm_w, w_ada, b_ada, w_in, b_gate, q_norm_a, k_norm_a, q_norm_b, k_norm_b, rel_bias,
           conv_w, conv_b, a_log, dt_bias, d_skip, ssm_norm_w, w_proj_a, w_proj_b, w_proj_c, w_out):
    bsz, s, d = x.shape
    depth = w_in.shape[0]
    c_pad = jnp.pad(c, ((0, 8 - bsz), (0, 0)))
    mod = _modulation(c_pad, w_ada, b_ada)[:, :bsz]
    cos_t, sin_t = _rope_tables(s)
    bias_tbl = _band_bias(rel_bias)
    tile2 = lambda v: jnp.tile(v.astype(F32), LANES // HEAD_DIM)
    for l in range(depth):
        shift, scale, gate = (mod[l, :, k * d:(k + 1) * d].reshape(bsz, 1, d) for k in range(3))
        w_main, w_t = _layer_weights(w_in[l])
        head_w = jnp.stack([tile2(q_norm_a[l]), tile2(k_norm_a[l]), tile2(q_norm_b[l]), tile2(k_norm_b[l])]
                           + [jnp.zeros((LANES,), F32)] * 4, axis=0)
        (qa, ka, vt, ga, *qkv_b, gb, xbc, zc, dt, mg) = _in_projection(
            x, 1.0 + scale, shift, norm_w[l].reshape(1, d), cos_t, sin_t, head_w,
            b_gate[l].reshape(1, -1), jnp.pad(conv_w[l], ((0, 8 - C_CONV), (0, 0))),
            conv_b[l].reshape(1, C_XBC), w_main, w_t)
        oa = _mixer_a(qa, ka, vt, _score_bound(q_norm_a[l], k_norm_a[l]))
        bound_b = (1.05 * HEAD_DIM ** 0.5 * jnp.max(jnp.abs(q_norm_b[l])) * jnp.max(jnp.abs(k_norm_b[l]))
                   + jnp.max(jnp.abs(rel_bias)))
        ob, lse = zip(*[_mixer_b_group(qkv_b[g], qkv_b[B_GROUPS + g], qkv_b[2 * B_GROUPS + g], bias_tbl[g],
                                       bound_b, g) for g in range(B_GROUPS)])
        prm = jnp.stack([_pad_row(dt_bias[l].reshape(-1)), _pad_row(a_log[l].reshape(-1)),
                         _pad_row(jnp.tile(d_skip[l], 2))] + [jnp.zeros((LANES,), F32)] * 5, axis=0)
        yf, yb = _ssd(xbc, dt, prm)
        perm_rows = lambda w: w.reshape(A_HEADS, HEAD_DIM, d)[A_HEAD_ORDER, :, :].reshape(A_Q, d)
        x = _merge(x, gate, oa, ga, ob, lse, gb, yf, yb, zc, mg, ssm_norm_w[l].reshape(1, -1),
                   perm_rows(w_proj_a[l]).astype(BF16), w_proj_b[l].astype(BF16),
                   w_proj_c[l].astype(BF16), w_out[l].astype(BF16))
    return x
```
